```python
import jax, jax.numpy as jnp
from jax import lax
import numpy as np

D_MODEL = 1024
BATCH = 8
SEQ = 8192
DEPTH = 1
DEC_BATCH = 128
DEC_SEQ = 8
PAST_LEN = 8192
PAGE_SIZE = 128

HEAD_DIM = 64
H_A = 8
H_B = 8
W_A = H_A * HEAD_DIM
W_B = H_B * HEAD_DIM
N_BRANCH = 2
D_IN = 4 * W_A + 4 * W_B + N_BRANCH * D_MODEL + H_B
MOBA_BLOCK = 256
MOBA_TOPK = 3
MOBA_Q_BLOCK = 32
FOX_Q_BLOCK = 128
ROPE_THETA = 10000.0
FORGET_BIAS_INIT = 4.0
EPS = 1e-6
NEG_INF = float('-inf')

kernel_name = 'moba_fox_gated_hybrid_step'


def rms_norm(x, gain):
    xf = x.astype(jnp.float32)
    y = xf * lax.rsqrt(jnp.mean(xf * xf, axis=-1, keepdims=True) + EPS)
    return (y * gain.astype(jnp.float32)).astype(x.dtype)


def rope(x, pos):
    half = HEAD_DIM // 2
    inv_freq = ROPE_THETA ** (-jnp.arange(half, dtype=jnp.float32) / half)
    ang = pos.astype(jnp.float32)[:, None] * inv_freq[None, :]
    cos = jnp.cos(ang)[None, :, None, :]
    sin = jnp.sin(ang)[None, :, None, :]
    xf = x.astype(jnp.float32)
    x1, x2 = xf[..., :half], xf[..., half:]
    return jnp.concatenate([x1 * cos - x2 * sin, x2 * cos + x1 * sin], axis=-1).astype(x.dtype)


def mixer_inputs(x, c, pos, norm_gain, w_ada, b_ada, w_in, b_forget, q_norm_a, k_norm_a, q_norm_b, k_norm_b):
    n, t, _ = x.shape
    mod = jax.nn.silu(c) @ w_ada + b_ada
    shift, scale, gate = jnp.split(mod, 3, axis=-1)
    h = rms_norm(x, norm_gain) * (1.0 + scale[:, None]) + shift[:, None]
    widths = (W_A,) * 4 + (W_B,) * 4 + (N_BRANCH * D_MODEL, H_B)
    cuts = [sum(widths[:i + 1]) for i in range(len(widths) - 1)]
    qa, ka, va, za, qb, kb, vb, zb, g_br, f_lin = jnp.split(h @ w_in, cuts, axis=-1)
    qa = rope(rms_norm(qa.reshape(n, t, H_A, HEAD_DIM), q_norm_a), pos)
    ka = rope(rms_norm(ka.reshape(n, t, H_A, HEAD_DIM), k_norm_a), pos)
    va = va.reshape(n, t, H_A, HEAD_DIM)
    qb = rms_norm(qb.reshape(n, t, H_B, HEAD_DIM), q_norm_b)
    kb = rms_norm(kb.reshape(n, t, H_B, HEAD_DIM), k_norm_b)
    vb = vb.reshape(n, t, H_B, HEAD_DIM)
    logf = jax.nn.log_sigmoid((f_lin + b_forget).astype(jnp.float32))
    branch_gate = jax.nn.sigmoid(g_br).reshape(n, t, N_BRANCH, D_MODEL)
    return qa, ka, va, za, qb, kb, vb, zb, logf, branch_gate, gate


def mixer_output(x, gate, oa, za, ob, zb, branch_gate, w_branch, w_out):
    n, t, _ = x.shape
    oa = oa.reshape(n, t, W_A) * jax.nn.silu(za)
    ob = ob.reshape(n, t, W_B) * jax.nn.silu(zb)
    o = jnp.stack([oa, ob], axis=2)
    u = jnp.einsum('ntgc,gcd->ntgd', o, w_branch)
    merged = jnp.sum(branch_gate * u, axis=2)
    return x + gate[:, None] * (merged @ w_out)


def moba_select(q, means, n_valid, nsel):
    s = jnp.einsum('nthd,njhd->nthj', q, means.astype(q.dtype)).astype(jnp.float32)
    s = jnp.where(jnp.arange(means.shape[1]) < n_valid, s, NEG_INF)
    _, idx = lax.top_k(s, nsel)
    valid = jnp.broadcast_to(jnp.arange(nsel) < n_valid, idx.shape)
    return idx, valid


def moba_core(q, k_sel, v_sel, sel_valid, k_own, v_own, own_mask):
    scale = HEAD_DIM ** -0.5
    s_own = jnp.einsum('nthd,nlhd->nthl', q, k_own).astype(jnp.float32) * scale
    s_own = jnp.where(own_mask[None, :, None, :], s_own, NEG_INF)
    if k_sel is None:
        p = jax.nn.softmax(s_own, axis=-1)
        return jnp.einsum('nthl,nlhd->nthd', p.astype(v_own.dtype), v_own)
    n, tq, h, ns, bs, _ = k_sel.shape
    s_sel = jnp.einsum('nthd,nthsbd->nthsb', q, k_sel).astype(jnp.float32) * scale
    s_sel = jnp.where(sel_valid[..., None], s_sel, NEG_INF).reshape(n, tq, h, ns * bs)
    p = jax.nn.softmax(jnp.concatenate([s_sel, s_own], axis=-1), axis=-1).astype(v_own.dtype)
    return (jnp.einsum('nthsb,nthsbd->nthd', p[..., :ns * bs].reshape(n, tq, h, ns, bs), v_sel)
            + jnp.einsum('nthl,nlhd->nthd', p[..., ns * bs:], v_own))


def moba_prompt(q, k, v):
    n, t = q.shape[:2]
    nb = -(-t // MOBA_BLOCK)
    pad = nb * MOBA_BLOCK - t
    kp = jnp.pad(k, ((0, 0), (0, pad), (0, 0), (0, 0)))
    vp = jnp.pad(v, ((0, 0), (0, pad), (0, 0), (0, 0)))
    kb = kp.reshape(n, nb, MOBA_BLOCK, H_A, HEAD_DIM)
    vb = vp.reshape(n, nb, MOBA_BLOCK, H_A, HEAD_DIM)
    means = jnp.mean(kb, axis=2, dtype=jnp.float32)
    nsel = min(MOBA_TOPK, nb - 1)
    n_idx = jnp.arange(n)[:, None, None, None]
    h_idx = jnp.arange(H_A)[None, None, :, None]

    def chunk(i):
        start = i * MOBA_Q_BLOCK
        blk = start // MOBA_BLOCK
        qc = lax.dynamic_slice_in_dim(q, start, MOBA_Q_BLOCK, axis=1)
        k_own = lax.dynamic_slice_in_dim(kp, blk * MOBA_BLOCK, MOBA_BLOCK, axis=1)
        v_own = lax.dynamic_slice_in_dim(vp, blk * MOBA_BLOCK, MOBA_BLOCK, axis=1)
        qpos = start + jnp.arange(MOBA_Q_BLOCK)
        kpos = blk * MOBA_BLOCK + jnp.arange(MOBA_BLOCK)
        own_mask = kpos[None, :] <= qpos[:, None]
        if nsel == 0:
            return moba_core(qc, None, None, None, k_own, v_own, own_mask)
        idx, valid = moba_select(qc, means, blk, nsel)
        k_sel = kb[n_idx, idx, :, h_idx, :]
        v_sel = vb[n_idx, idx, :, h_idx, :]
        return moba_core(qc, k_sel, v_sel, valid, k_own, v_own, own_mask)

    out = lax.map(chunk, jnp.arange(t // MOBA_Q_BLOCK))
    return out.swapaxes(0, 1).reshape(n, t, H_A, HEAD_DIM)


def moba_sample(q, k_new, v_new, cache_k, cache_v, page_means, page_table, layer):
    n, t = q.shape[:2]
    ppb = MOBA_BLOCK // PAGE_SIZE
    nbf = PAST_LEN // MOBA_BLOCK
    blk_start = nbf * MOBA_BLOCK
    r = PAST_LEN - blk_start
    own_pages = page_table[:, blk_start // PAGE_SIZE: PAST_LEN // PAGE_SIZE]
    k_own = jnp.concatenate([cache_k[layer, own_pages].reshape(n, r, H_A, HEAD_DIM), k_new], axis=1)
    v_own = jnp.concatenate([cache_v[layer, own_pages].reshape(n, r, H_A, HEAD_DIM), v_new], axis=1)
    kpos = blk_start + jnp.arange(r + t)
    qpos = PAST_LEN + jnp.arange(t)
    own_mask = kpos[None, :] <= qpos[:, None]
    nsel = min(MOBA_TOPK, nbf)
    if nsel == 0:
        return moba_core(q, None, None, None, k_own, v_own, own_mask)
    means = page_means[page_table[:, :nbf * ppb]].reshape(n, nbf, ppb, H_A, HEAD_DIM).mean(axis=2)
    idx, valid = moba_select(q, means, nbf, nsel)
    logical = idx[..., None] * ppb + jnp.arange(ppb)
    phys = page_table[jnp.arange(n)[:, None, None, None, None], logical]
    h_idx = jnp.arange(H_A)[None, :, None, None]

    def one_token(xs):
        q_t, phys_t, valid_t, mask_t = xs
        k_sel = cache_k[layer, phys_t, :, h_idx, :].reshape(n, 1, H_A, nsel, MOBA_BLOCK, HEAD_DIM)
        v_sel = cache_v[layer, phys_t, :, h_idx, :].reshape(n, 1, H_A, nsel, MOBA_BLOCK, HEAD_DIM)
        return moba_core(q_t[:, None], k_sel, v_sel, valid_t[:, None], k_own, v_own, mask_t[None])[:, 0]

    out = lax.map(one_token, (q.swapaxes(0, 1), phys.swapaxes(0, 1), valid.swapaxes(0, 1), own_mask))
    return out.swapaxes(0, 1)


def fox_prompt(q, k, v, logf):
    n, t = q.shape[:2]
    scale = HEAD_DIM ** -0.5
    cum = jnp.cumsum(logf.astype(jnp.float32), axis=1).transpose(0, 2, 1)
    kpos = jnp.arange(t)

    def block(i):
        start = i * FOX_Q_BLOCK
        qc = lax.dynamic_slice_in_dim(q, start, FOX_Q_BLOCK, axis=1)
        cq = lax.dynamic_slice_in_dim(cum, start, FOX_Q_BLOCK, axis=2)
        s = jnp.einsum('nqhd,nkhd->nhqk', qc, k).astype(jnp.float32) * scale
        s = s + cq[..., :, None] - cum[..., None, :]
        qpos = start + jnp.arange(FOX_Q_BLOCK)
        s = jnp.where(kpos[None, :] <= qpos[:, None], s, NEG_INF)
        p = jax.nn.softmax(s, axis=-1)
        return jnp.einsum('nhqk,nkhd->nqhd', p.astype(v.dtype), v)

    out = lax.map(block, jnp.arange(t // FOX_Q_BLOCK))
    return out.swapaxes(0, 1).reshape(n, t, H_B, HEAD_DIM)


def fox_sample(q, k_new, v_new, logf_new, cache_k, cache_v, cache_logf, page_table, layer):
    n, t = q.shape[:2]
    n_pages = PAST_LEN // PAGE_SIZE
    scale = HEAD_DIM ** -0.5
    lf_past = cache_logf[layer, page_table].astype(jnp.float32).reshape(n, PAST_LEN, H_B)
    cum_past = jnp.cumsum(lf_past, axis=1)
    cum_new = cum_past[:, -1:, :] + jnp.cumsum(logf_new.astype(jnp.float32), axis=1)
    cq = cum_new.transpose(0, 2, 1)
    s = jnp.einsum('nqhd,nkhd->nhqk', q, k_new).astype(jnp.float32) * scale + cq[..., :, None] - cq[..., None, :]
    s = jnp.where(jnp.arange(t)[None, :] <= jnp.arange(t)[:, None], s, NEG_INF)
    m = s.max(axis=-1)
    p = jnp.exp(s - m[..., None])
    l = p.sum(axis=-1)
    acc = jnp.einsum('nhqk,nkhd->nhqd', p, v_new.astype(jnp.float32))
    cum_pages = cum_past.reshape(n, n_pages, PAGE_SIZE, H_B).transpose(1, 0, 3, 2)

    def page_step(carry, xs):
        m, l, acc = carry
        phys, cp = xs
        kp = cache_k[layer, phys]
        vp = cache_v[layer, phys]
        s = jnp.einsum('nqhd,nkhd->nhqk', q, kp).astype(jnp.float32) * scale + cq[..., :, None] - cp[:, :, None, :]
        m_new = jnp.maximum(m, s.max(axis=-1))
        corr = jnp.exp(m - m_new)
        p = jnp.exp(s - m_new[..., None])
        acc = acc * corr[..., None] + jnp.einsum('nhqk,nkhd->nhqd', p, vp.astype(jnp.float32))
        return (m_new, l * corr + p.sum(axis=-1), acc), None

    (m, l, acc), _ = lax.scan(page_step, (m, l, acc), (page_table.T, cum_pages))
    return (acc / l[..., None]).transpose(0, 2, 1, 3).astype(q.dtype)


def setup_inputs(seed: int = 0) -> dict:
    key = jax.random.key(seed)
    ks = jax.random.split(key, 24)
    f32 = jnp.float32
    n_pages = PAST_LEN // PAGE_SIZE
    n_pool = (DEC_BATCH * n_pages * 5) // 4

    def nrm(k, shape, s=1.0):
        return s * jax.random.normal(k, shape, f32)

    page_table = jax.random.permutation(ks[7], n_pool)[: DEC_BATCH * n_pages].reshape(DEC_BATCH, n_pages).astype(jnp.int32)
    return {
        'x_prompt': nrm(ks[0], (BATCH, SEQ, D_MODEL)),
        'x_sample': nrm(ks[1], (DEC_BATCH, DEC_SEQ, D_MODEL)),
        'cache_moba_k': nrm(ks[2], (DEPTH, n_pool, PAGE_SIZE, H_A, HEAD_DIM)),
        'cache_moba_v': nrm(ks[3], (DEPTH, n_pool, PAGE_SIZE, H_A, HEAD_DIM)),
        'cache_fox_k': nrm(ks[4], (DEPTH, n_pool, PAGE_SIZE, H_B, HEAD_DIM)),
        'cache_fox_v': nrm(ks[5], (DEPTH, n_pool, PAGE_SIZE, H_B, HEAD_DIM)),
        'cache_fox_logf': jax.nn.log_sigmoid(FORGET_BIAS_INIT + nrm(ks[6], (DEPTH, n_pool, PAGE_SIZE, H_B))),
        'page_table': page_table,
        'c_prompt': nrm(ks[8], (BATCH, D_MODEL)),
        'c_sample': nrm(ks[9], (DEC_BATCH, D_MODEL)),
        'norm_gain': 1.0 + nrm(ks[10], (DEPTH, D_MODEL), 0.02),
        'w_ada': nrm(ks[11], (DEPTH, D_MODEL, 3 * D_MODEL), D_MODEL ** -0.5),
        'b_ada': nrm(ks[12], (DEPTH, 3 * D_MODEL), 0.02),
        'w_in': nrm(ks[13], (DEPTH, D_MODEL, D_IN), D_MODEL ** -0.5),
        'b_forget': FORGET_BIAS_INIT + nrm(ks[14], (DEPTH, H_B), 0.1),
        'q_norm_a': 1.0 + nrm(ks[15], (DEPTH, HEAD_DIM), 0.02),
        'k_norm_a': 1.0 + nrm(ks[16], (DEPTH, HEAD_DIM), 0.02),
        'q_norm_b': 1.0 + nrm(ks[17], (DEPTH, HEAD_DIM), 0.02),
        'k_norm_b': 1.0 + nrm(ks[18], (DEPTH, HEAD_DIM), 0.02),
        'w_branch': nrm(ks[19], (DEPTH, N_BRANCH, W_A, D_MODEL), W_A ** -0.5),
        'w_out': nrm(ks[20], (DEPTH, D_MODEL, D_MODEL), D_MODEL ** -0.5),
    }


def reference(x_prompt, x_sample, cache_moba_k, cache_moba_v, cache_fox_k, cache_fox_v, cache_fox_logf,
              page_table, c_prompt, c_sample, norm_gain, w_ada, b_ada, w_in, b_forget,
              q_norm_a, k_norm_a, q_norm_b, k_norm_b, w_branch, w_out):
    pos_p = jnp.arange(SEQ)
    pos_s = PAST_LEN + jnp.arange(DEC_SEQ)
    page_means = jnp.mean(cache_moba_k, axis=2, dtype=jnp.float32)
    yp, ys = x_prompt, x_sample
    mk_p, mv_p, fk_p, fv_p, fl_p = [], [], [], [], []
    mk_s, mv_s, fk_s, fv_s, fl_s = [], [], [], [], []
    for l in range(DEPTH):
        lw = (norm_gain[l], w_ada[l], b_ada[l], w_in[l], b_forget[l], q_norm_a[l], k_norm_a[l], q_norm_b[l], k_norm_b[l])
        qa, ka, va, za, qb, kb, vb, zb, lf, bg, g = mixer_inputs(yp, c_prompt, pos_p, *lw)
        oa = moba_prompt(qa, ka, va)
        ob = fox_prompt(qb, kb, vb, lf)
        yp = mixer_output(yp, g, oa, za, ob, zb, bg, w_branch[l], w_out[l])
        mk_p.append(ka); mv_p.append(va); fk_p.append(kb); fv_p.append(vb); fl_p.append(lf)
        qa, ka, va, za, qb, kb, vb, zb, lf, bg, g = mixer_inputs(ys, c_sample, pos_s, *lw)
        oa = moba_sample(qa, ka, va, cache_moba_k, cache_moba_v, page_means[l], page_table, l)
        ob = fox_sample(qb, kb, vb, lf, cache_fox_k, cache_fox_v, cache_fox_logf, page_table, l)
        ys = mixer_output(ys, g, oa, za, ob, zb, bg, w_branch[l], w_out[l])
        mk_s.append(ka); mv_s.append(va); fk_s.append(kb); fv_s.append(vb); fl_s.append(lf)
    return (yp, ys,
            jnp.stack(mk_p), jnp.stack(mv_p), jnp.stack(fk_p), jnp.stack(fv_p), jnp.stack(fl_p),
            jnp.stack(mk_s), jnp.stack(mv_s), jnp.stack(fk_s), jnp.stack(fv_s), jnp.stack(fl_s))
```

```python
import functools

import jax
import jax.numpy as jnp
from jax import lax
from jax.experimental import pallas as pl
from jax.experimental.pallas import tpu as pltpu

F32 = jnp.float32
BF16 = jnp.bfloat16
HIGHEST = lax.Precision.HIGHEST
NEG_INF = float("-inf")

HEAD_DIM = 64
N_HEADS = 8
WIDTH = N_HEADS * HEAD_DIM
N_BRANCH = 2
MOBA_BLOCK = 256
MOBA_TOPK = 3
PAGE_SIZE = 128
ROPE_THETA = 10000.0
EPS = 1e-6
LANES = 128
ROW_TILE = 256
MiB = 1024 * 1024


def _cparams(dims, vmem_mib):
    return pltpu.CompilerParams(dimension_semantics=dims, vmem_limit_bytes=vmem_mib * MiB)


def _dot(a, b, precision=None):
    return jnp.dot(a, b, preferred_element_type=F32, precision=precision)


def _dot_nt(a, b, precision=None):
    return lax.dot_general(a, b, (((1,), (1,)), ((), ())), preferred_element_type=F32, precision=precision)


def _iota(shape, dim):
    return lax.broadcasted_iota(jnp.int32, shape, dim)


def _ada_kernel(c_ref, w_ref, b_ref, o_ref):
    c = c_ref[...]
    a = c * jax.nn.sigmoid(c)
    o_ref[...] = _dot(a, w_ref[...], HIGHEST) + b_ref[...]


def _ada_mod(c, w_ada, b_ada):
    n, d = c.shape
    d3 = w_ada.shape[1]
    tn = 512
    return pl.pallas_call(
        _ada_kernel,
        grid=(d3 // tn,),
        in_specs=[pl.BlockSpec((n, d), lambda j: (0, 0)),
                  pl.BlockSpec((d, tn), lambda j: (0, j)),
                  pl.BlockSpec((1, tn), lambda j: (0, j))],
        out_specs=pl.BlockSpec((n, tn), lambda j: (0, j)),
        out_shape=jax.ShapeDtypeStruct((n, d3), F32),
        compiler_params=_cparams(("parallel",), 32),
        name="ada_mod",
    )(c, w_ada, b_ada.reshape(1, d3))


def _log_sigmoid(v):
    return jnp.minimum(v, 0.0) - jnp.log1p(jnp.exp(-jnp.abs(v)))


def _proj_kernel(prompt, tiles_per_seq, *refs):
    (x_ref, shift_ref, scale_ref, gain_ref, w_ref, bf_ref, gqa_ref, gka_ref, gqb_ref, gkb_ref,
     cos_ref, sin_ref, gsum_ref) = refs[:13]
    if prompt:
        (kat_ref, vat_ref, kbt_ref, vbt_ref, logft_ref, qa_ref, qb_ref, kab_ref, kbb_ref, vatb_ref, vbtb_ref,
         za_ref, zb_ref, bg_ref, selb_ref, crep_ref, means_ref, carry_ref) = refs[13:]
    else:
        (ka_ref, va_ref, kb_ref, vb_ref, logf_ref, qa_ref, qb_ref, za_ref, zb_ref, bg_ref) = refs[13:]
    tm = x_ref.shape[0]
    it = pl.program_id(0) % tiles_per_seq

    x = x_ref[...]
    ms = jnp.mean(x * x, axis=-1, keepdims=True)
    h = x * lax.rsqrt(ms + EPS) * gain_ref[...]
    h = h * (1.0 + scale_ref[0]) + shift_ref[0]
    hb = h.astype(BF16)

    def seg(c0, width=WIDTH):
        return _dot(hb, w_ref[:, c0:c0 + width])

    def head_norm(z, g_ref):
        ss = _dot((z * z).astype(BF16), gsum_ref[...])
        return z * lax.rsqrt(ss * (1.0 / HEAD_DIM) + EPS) * g_ref[...]

    cosf = jnp.concatenate([cos_ref[...]] * (WIDTH // LANES), axis=1)
    sinf = jnp.concatenate([sin_ref[...]] * (WIDTH // LANES), axis=1)
    first_half = (_iota((tm, WIDTH), 1) & (HEAD_DIM - 1)) < (HEAD_DIM // 2)

    def rope(y):
        partner = jnp.where(first_half, pltpu.roll(y, WIDTH - HEAD_DIM // 2, 1), pltpu.roll(y, HEAD_DIM // 2, 1))
        return y * cosf + partner * sinf

    qa = rope(head_norm(seg(0), gqa_ref))
    ka = rope(head_norm(seg(WIDTH), gka_ref))
    va = seg(2 * WIDTH)
    za = seg(3 * WIDTH)
    za_ref[...] = (za * jax.nn.sigmoid(za)).astype(BF16)
    qb = head_norm(seg(4 * WIDTH), gqb_ref)
    kb = head_norm(seg(5 * WIDTH), gkb_ref)
    vb = seg(6 * WIDTH)
    zb = seg(7 * WIDTH)
    zb_ref[...] = (zb * jax.nn.sigmoid(zb)).astype(BF16)
    for s in range(2 * N_BRANCH):
        g = seg(8 * WIDTH + s * WIDTH)
        bg_ref[:, s * WIDTH:(s + 1) * WIDTH] = jax.nn.sigmoid(g).astype(BF16)
    lf = _log_sigmoid(seg(12 * WIDTH, LANES) + bf_ref[...])

    if not prompt:
        ka_ref[...] = ka
        va_ref[...] = va
        kb_ref[...] = kb
        vb_ref[...] = vb
        logf_ref[...] = lf[:, :N_HEADS]
        qa_ref[...] = qa
        qb_ref[...] = qb
        return

    scale = HEAD_DIM ** -0.5
    qa_ref[...] = (qa * scale).astype(BF16)
    qb_ref[...] = (qb * scale).astype(BF16)
    kab_ref[...] = ka.astype(BF16)
    kbb_ref[...] = kb.astype(BF16)
    kat_ref[0] = ka.T
    kbt_ref[0] = kb.T
    va_t = va.T
    vb_t = vb.T
    vat_ref[0] = va_t
    vbt_ref[0] = vb_t
    vatb_ref[0] = va_t.astype(BF16)
    vbtb_ref[0] = vb_t.astype(BF16)
    logft_ref[0] = lf.T[:N_HEADS, :]

    nblk = means_ref.shape[0]

    @pl.when(it == 0)
    def _():
        means_ref[...] = jnp.zeros_like(means_ref)
        carry_ref[...] = jnp.zeros_like(carry_ref)

    tri = (_iota((tm, tm), 1) <= _iota((tm, tm), 0)).astype(F32)
    cum = _dot(tri, lf, HIGHEST) + carry_ref[...]
    carry_ref[...] = cum[tm - 1:tm, :]
    for hh in range(N_HEADS):
        crep_ref[:, hh * LANES:(hh + 1) * LANES] = jnp.broadcast_to(cum[:, hh:hh + 1], (tm, LANES))

    mt = means_ref[...]
    mbd = jnp.concatenate([mt] * N_HEADS, axis=0)
    rr = _iota(mbd.shape, 0)
    ll = _iota(mbd.shape, 1)
    head_of_row = jnp.zeros(mbd.shape, jnp.int32)
    for hh in range(1, N_HEADS):
        head_of_row = head_of_row + (rr >= hh * nblk).astype(jnp.int32)
    mbd = jnp.where(head_of_row == (ll >> 6), mbd, 0.0)
    st = _dot_nt(mbd, qa, HIGHEST)
    jj = _iota((nblk, tm), 0)
    jf = jj.astype(F32)
    valid = jj < it
    for hh in range(N_HEADS):
        s = jnp.where(valid, st[hh * nblk:(hh + 1) * nblk, :], NEG_INF)
        sel = jnp.zeros((nblk, tm), F32)
        for _ in range(MOBA_TOPK):
            m = jnp.max(s, axis=0, keepdims=True)
            idx = jnp.min(jnp.where(s == m, jf, float(nblk)), axis=0, keepdims=True)
            pick = jf == idx
            sel = jnp.where(pick & valid, 1.0, sel)
            s = jnp.where(pick, NEG_INF, s)
        selb_ref[0, 0, hh * nblk:(hh + 1) * nblk, :] = jnp.where(sel > 0.5, 0.0, NEG_INF)
    means_ref[pl.ds(it, 1), :] = jnp.mean(ka, axis=0, keepdims=True)


def _proj_in(prompt, x, shift, scale, gain, w_pad, bf_pad, gqa, gka, gqb, gkb, cos_t, sin_t, gsum, n_seq, seq_len):
    m, d = x.shape
    tm = ROW_TILE
    n_tiles = m // tm
    if prompt:
        tiles_per_seq = seq_len // tm
        tiles_per_mod = tiles_per_seq
    else:
        tiles_per_seq = 1
        tiles_per_mod = 1
    r = shift.shape[1]
    tab_tiles = cos_t.shape[0] // tm
    wp = w_pad.shape[1]
    row = lambda width: pl.BlockSpec((tm, width), lambda i: (i, 0))
    const = lambda shape: pl.BlockSpec(shape, lambda i: (0,) * len(shape))
    in_specs = [
        row(d),
        pl.BlockSpec((1, r, d), lambda i: (i // tiles_per_mod, 0, 0)),
        pl.BlockSpec((1, r, d), lambda i: (i // tiles_per_mod, 0, 0)),
        const((1, d)),
        const((d, wp)),
        const((1, LANES)),
        const((1, WIDTH)), const((1, WIDTH)), const((1, WIDTH)), const((1, WIDTH)),
        pl.BlockSpec((tm, LANES), lambda i: (i % tab_tiles, 0)),
        pl.BlockSpec((tm, LANES), lambda i: (i % tab_tiles, 0)),
        const((WIDTH, WIDTH)),
    ]
    f32_tok = jax.ShapeDtypeStruct((m, WIDTH), F32)
    bf_tok = jax.ShapeDtypeStruct((m, WIDTH), BF16)
    scratch = []
    if prompt:
        nblk = seq_len // MOBA_BLOCK
        vt = jax.ShapeDtypeStruct((n_seq, WIDTH, seq_len), BF16)
        ft = jax.ShapeDtypeStruct((n_seq, WIDTH, seq_len), F32)
        vt_spec = pl.BlockSpec((1, WIDTH, tm), lambda i: (i // tiles_per_seq, 0, i % tiles_per_seq))
        out_shape = [ft, ft, ft, ft, jax.ShapeDtypeStruct((n_seq, N_HEADS, seq_len), F32)]
        out_specs = [vt_spec, vt_spec, vt_spec, vt_spec,
                     pl.BlockSpec((1, N_HEADS, tm), lambda i: (i // tiles_per_seq, 0, i % tiles_per_seq))]
        out_shape += [bf_tok, bf_tok, bf_tok, bf_tok, vt, vt, bf_tok, bf_tok,
                      jax.ShapeDtypeStruct((m, 2 * N_BRANCH * WIDTH), BF16),
                      jax.ShapeDtypeStruct((n_seq, nblk, N_HEADS * nblk, tm), F32),
                      jax.ShapeDtypeStruct((m, N_HEADS * LANES), F32)]
        out_specs += [row(WIDTH), row(WIDTH), row(WIDTH), row(WIDTH), vt_spec, vt_spec, row(WIDTH), row(WIDTH),
                      row(2 * N_BRANCH * WIDTH),
                      pl.BlockSpec((1, 1, N_HEADS * nblk, tm), lambda i: (i // tiles_per_seq, i % tiles_per_seq, 0, 0)),
                      row(N_HEADS * LANES)]
        scratch = [pltpu.VMEM((nblk, WIDTH), F32), pltpu.VMEM((1, LANES), F32)]
    else:
        out_shape = [f32_tok, f32_tok, f32_tok, f32_tok, jax.ShapeDtypeStruct((m, N_HEADS), F32),
                     f32_tok, f32_tok, bf_tok, bf_tok, jax.ShapeDtypeStruct((m, 2 * N_BRANCH * WIDTH), BF16)]
        out_specs = [row(WIDTH), row(WIDTH), row(WIDTH), row(WIDTH), row(N_HEADS),
                     row(WIDTH), row(WIDTH), row(WIDTH), row(WIDTH), row(2 * N_BRANCH * WIDTH)]
    return pl.pallas_call(
        functools.partial(_proj_kernel, prompt, tiles_per_seq),
        grid=(n_tiles,),
        in_specs=in_specs,
        out_specs=out_specs,
        out_shape=out_shape,
        scratch_shapes=scratch,
        compiler_params=_cparams(("arbitrary",), 56),
        name="proj_in_prompt" if prompt else "proj_in_sample",
    )(x, shift, scale, gain, w_pad, bf_pad, gqa, gka, gqb, gkb, cos_t, sin_t, gsum)


def _attn_prompt_kernel(fox, q_ref, k_ref, vt_ref, b_ref, o_ref):
    i = pl.program_id(2)
    tq = q_ref.shape[1]
    bk = tq
    q = q_ref[0]
    lane = _iota(q.shape, 1)
    zero = jnp.zeros_like(q)
    qm = (jnp.where(lane < HEAD_DIM, q, zero), jnp.where(lane >= HEAD_DIM, q, zero))
    nblk = b_ref.shape[2] // 2 if not fox else 0

    def scores(a, j, diag):
        j0 = pl.multiple_of(j * bk, bk)
        s = _dot_nt(k_ref[0, pl.ds(j0, bk), :], qm[a])
        if fox:
            c = b_ref[0, pl.ds(j0, bk), a * LANES:(a + 1) * LANES]
            s = s - jnp.concatenate([c] * (tq // LANES), axis=1)
        elif not diag:
            s = s + b_ref[0, 0, pl.ds(a * nblk + j, 1), :]
        if diag:
            s = jnp.where(_iota(s.shape, 0) <= _iota(s.shape, 1), s, NEG_INF)
        return s

    def update(state, j, diag):
        out = []
        j0 = pl.multiple_of(j * bk, bk)
        vblk = vt_ref[0, :, pl.ds(j0, bk)]
        for a in range(2):
            m, l, acc = state[a]
            s = scores(a, j, diag)
            m_new = jnp.maximum(m, jnp.max(s, axis=0, keepdims=True))
            alpha = jnp.exp(m - m_new)
            p = jnp.exp(s - m_new)
            l = alpha * l + jnp.sum(p, axis=0, keepdims=True)
            acc = alpha * acc + _dot(vblk, p.astype(BF16))
            out.append((m_new, l, acc))
        return tuple(out)

    init = tuple((jnp.full((1, tq), NEG_INF, F32), jnp.zeros((1, tq), F32), jnp.zeros((LANES, tq), F32))
                 for _ in range(2))
    state = update(init, i, True)
    state = lax.fori_loop(0, i, lambda j, st: update(st, j, False), state)
    o0 = state[0][2] / state[0][1]
    o1 = state[1][2] / state[1][1]
    ot = jnp.where(_iota(o0.shape, 0) < HEAD_DIM, o0, o1)
    o_ref[0] = ot.T.astype(o_ref.dtype)


def _attn_prompt(fox, q, k, vt, bias):
    n, t, _ = q.shape
    tq = MOBA_BLOCK
    if fox:
        b_spec = pl.BlockSpec((1, t, 2 * LANES), lambda b, hp, i: (b, 0, hp))
    else:
        nblk = t // MOBA_BLOCK
        b_spec = pl.BlockSpec((1, 1, 2 * nblk, tq), lambda b, hp, i: (b, i, hp, 0))
    return pl.pallas_call(
        functools.partial(_attn_prompt_kernel, fox),
        grid=(n, N_HEADS // 2, t // tq),
        in_specs=[pl.BlockSpec((1, tq, LANES), lambda b, hp, i: (b, i, hp)),
                  pl.BlockSpec((1, t, LANES), lambda b, hp, i: (b, 0, hp)),
                  pl.BlockSpec((1, LANES, t), lambda b, hp, i: (b, hp, 0)),
                  b_spec],
        out_specs=pl.BlockSpec((1, tq, LANES), lambda b, hp, i: (b, i, hp)),
        out_shape=jax.ShapeDtypeStruct((n, t, WIDTH), BF16),
        compiler_params=_cparams(("parallel", "parallel", "arbitrary"), 56),
        name="fox_prompt_attn" if fox else "moba_prompt_attn",
    )(q, k, vt, bias)


def _out_kernel(oa_ref, ob_ref, za_ref, zb_ref, bg_ref, x_ref, gate_ref, wb_ref, wo_ref, y_ref):
    d = x_ref.shape[1]
    ga = (oa_ref[...].astype(F32) * za_ref[...].astype(F32)).astype(BF16)
    gb = (ob_ref[...].astype(F32) * zb_ref[...].astype(F32)).astype(BF16)
    ua = _dot(ga, wb_ref[0])
    ub = _dot(gb, wb_ref[1])
    merged = bg_ref[:, :d].astype(F32) * ua + bg_ref[:, d:].astype(F32) * ub
    y_ref[...] = x_ref[...] + gate_ref[0] * _dot(merged.astype(BF16), wo_ref[...])


def _mixer_out(oa, ob, za, zb, bg, x, gate, wb, wo, tiles_per_mod):
    m, d = x.shape
    tm = ROW_TILE
    r = gate.shape[1]
    row = lambda width: pl.BlockSpec((tm, width), lambda i: (i, 0))
    return pl.pallas_call(
        _out_kernel,
        grid=(m // tm,),
        in_specs=[row(WIDTH), row(WIDTH), row(WIDTH), row(WIDTH), row(N_BRANCH * d), row(d),
                  pl.BlockSpec((1, r, d), lambda i: (i // tiles_per_mod, 0, 0)),
                  pl.BlockSpec((N_BRANCH, WIDTH, d), lambda i: (0, 0, 0)),
                  pl.BlockSpec((d, d), lambda i: (0, 0))],
        out_specs=row(d),
        out_shape=jax.ShapeDtypeStruct((m, d), F32),
        compiler_params=_cparams(("parallel",), 48),
        name="mixer_out",
    )(oa, ob, za, zb, bg, x, gate, wb, wo)


def _block_diag_rows(q):
    t_new = q.shape[0]
    keep = (_iota((N_HEADS, WIDTH), 1) >> 6) == _iota((N_HEADS, WIDTH), 0)
    return jnp.concatenate([jnp.where(keep, q[t:t + 1, :], 0.0) for t in range(t_new)], axis=0)


def _collapse_rows(o64):
    rows = o64.shape[0]
    keep = (_iota((rows, WIDTH), 1) >> 6) == (_iota((rows, WIDTH), 0) & (N_HEADS - 1))
    om = jnp.where(keep, o64, 0.0)
    return jnp.concatenate([jnp.sum(om[t * N_HEADS:(t + 1) * N_HEADS], axis=0, keepdims=True)
                            for t in range(rows // N_HEADS)], axis=0)


def _pad_rows(a, rows):
    return jnp.concatenate([a, jnp.zeros((rows - a.shape[0], a.shape[1]), a.dtype)], axis=0)


def _new_token_init(qbd, kn_ref, vn_ref, bias, m_ref, l_ref, acc_ref):
    kn = _pad_rows(kn_ref[0], LANES).astype(BF16)
    vn = _pad_rows(vn_ref[0], LANES).astype(BF16)
    s = _dot_nt(qbd, kn)
    if bias is not None:
        s = s + bias
    s = jnp.where(_iota(s.shape, 1) <= (_iota(s.shape, 0) >> 3), s, NEG_INF)
    m = jnp.max(s, axis=1, keepdims=True)
    p = jnp.exp(s - m)
    m_ref[...] = m
    l_ref[...] = jnp.sum(p, axis=1, keepdims=True)
    acc_ref[...] = _dot(p.astype(BF16), vn)


def _online_update(s, vt, m_ref, l_ref, acc_ref):
    m_old = m_ref[...]
    m_new = jnp.maximum(m_old, jnp.max(s, axis=1, keepdims=True))
    alpha = jnp.exp(m_old - m_new)
    p = jnp.exp(s - m_new)
    m_ref[...] = m_new
    l_ref[...] = alpha * l_ref[...] + jnp.sum(p, axis=1, keepdims=True)
    acc_ref[...] = alpha * acc_ref[...] + _dot_nt(p.astype(BF16), vt)


def _page_spec(pages_per_seq, group, g):
    return pl.BlockSpec((1, WIDTH, PAGE_SIZE),
                        lambda b, s, pt: (pt[b * pages_per_seq + s * group + g], 0, 0))


def _pages(page_refs):
    return jnp.concatenate([r[0] for r in page_refs], axis=1)


def _moba_select_kernel(group, n_blocks, pt_ref, q_ref, *refs):
    k_refs = refs[:group]
    sel_ref = refs[group]
    bm_ref = refs[group + 1]
    s = pl.program_id(1)
    ppb = MOBA_BLOCK // PAGE_SIZE

    @pl.when(s == 0)
    def _():
        bm_ref[...] = jnp.zeros_like(bm_ref)

    lane_blk = _iota(bm_ref.shape, 1)
    for jb in range(group // ppb):
        tot = k_refs[ppb * jb][0]
        for e in range(1, ppb):
            tot = tot + k_refs[ppb * jb + e][0]
        mean = jnp.sum(tot, axis=1, keepdims=True) * (1.0 / MOBA_BLOCK)
        bm_ref[...] = jnp.where(lane_blk == s * (group // ppb) + jb, mean, bm_ref[...])

    @pl.when(s == pl.num_programs(1) - 1)
    def _():
        qbd = _block_diag_rows(q_ref[0])
        sc = _dot(qbd, bm_ref[...], HIGHEST)
        lane = _iota(sc.shape, 1)
        lf = lane.astype(F32)
        valid = lane < n_blocks
        sc = jnp.where(valid, sc, NEG_INF)
        sel = jnp.zeros(sc.shape, F32)
        for _ in range(min(MOBA_TOPK, n_blocks)):
            m = jnp.max(sc, axis=1, keepdims=True)
            idx = jnp.min(jnp.where(sc == m, lf, float(LANES)), axis=1, keepdims=True)
            pick = lf == idx
            sel = jnp.where(pick & valid, 1.0, sel)
            sc = jnp.where(pick, NEG_INF, sc)
        sel_ref[0] = sel


def _moba_select(q, cache_k, pt_flat, pages_per_seq, group):
    b, t_new, _ = q.shape
    n_blocks = pages_per_seq * PAGE_SIZE // MOBA_BLOCK
    steps = pages_per_seq // group
    grid_spec = pltpu.PrefetchScalarGridSpec(
        num_scalar_prefetch=1,
        grid=(b, steps),
        in_specs=[pl.BlockSpec((1, t_new, WIDTH), lambda bb, s, pt: (bb, 0, 0))]
        + [_page_spec(pages_per_seq, group, g) for g in range(group)],
        out_specs=pl.BlockSpec((1, t_new * N_HEADS, LANES), lambda bb, s, pt: (bb, 0, 0)),
        scratch_shapes=[pltpu.VMEM((WIDTH, LANES), F32)],
    )
    return pl.pallas_call(
        functools.partial(_moba_select_kernel, group, n_blocks),
        grid_spec=grid_spec,
        out_shape=jax.ShapeDtypeStruct((b, t_new * N_HEADS, LANES), F32),
        compiler_params=_cparams(("parallel", "arbitrary"), 48),
        name="moba_sample_select",
    )(pt_flat, q, *([cache_k] * group))


def _moba_sample_kernel(group, pt_ref, q_ref, kn_ref, vn_ref, sel_ref, *refs):
    k_refs = refs[:group]
    v_refs = refs[group:2 * group]
    o_ref = refs[2 * group]
    m_ref, l_ref, acc_ref = refs[2 * group + 1:]
    s = pl.program_id(1)
    qbd = _block_diag_rows(q_ref[0] * HEAD_DIM ** -0.5).astype(BF16)

    @pl.when(s == 0)
    def _():
        _new_token_init(qbd, kn_ref, vn_ref, None, m_ref, l_ref, acc_ref)

    kt = _pages(k_refs).astype(BF16)
    vt = _pages(v_refs).astype(BF16)
    sc = _dot(qbd, kt)
    keys = group * PAGE_SIZE
    blocks_per_step = keys // MOBA_BLOCK
    blk_of_key = s * blocks_per_step + (_iota((LANES, keys), 1) >> 8)
    expand = (_iota((LANES, keys), 0) == blk_of_key).astype(BF16)
    chosen = _dot(sel_ref[0].astype(BF16), expand)
    sc = jnp.where(chosen > 0.5, sc, NEG_INF)
    _online_update(sc, vt, m_ref, l_ref, acc_ref)

    @pl.when(s == pl.num_programs(1) - 1)
    def _():
        o_ref[0] = _collapse_rows(acc_ref[...] / l_ref[...])


def _sample_scratch(t_new):
    rows = t_new * N_HEADS
    return [pltpu.VMEM((rows, 1), F32), pltpu.VMEM((rows, 1), F32), pltpu.VMEM((rows, WIDTH), F32)]


def _moba_sample(q, k_new, v_new, sel, cache_k, cache_v, pt_flat, pages_per_seq, group):
    b, t_new, _ = q.shape
    steps = pages_per_seq // group
    tok = pl.BlockSpec((1, t_new, WIDTH), lambda bb, s, pt: (bb, 0, 0))
    grid_spec = pltpu.PrefetchScalarGridSpec(
        num_scalar_prefetch=1,
        grid=(b, steps),
        in_specs=[tok, tok, tok, pl.BlockSpec((1, t_new * N_HEADS, LANES), lambda bb, s, pt: (bb, 0, 0))]
        + [_page_spec(pages_per_seq, group, g) for g in range(group)] * 2,
        out_specs=tok,
        scratch_shapes=_sample_scratch(t_new),
    )
    return pl.pallas_call(
        functools.partial(_moba_sample_kernel, group),
        grid_spec=grid_spec,
        out_shape=jax.ShapeDtypeStruct((b, t_new, WIDTH), F32),
        compiler_params=_cparams(("parallel", "arbitrary"), 56),
        name="moba_sample_attn",
    )(pt_flat, q, k_new, v_new, sel, *([cache_k] * group), *([cache_v] * group))


def _suffix_kernel(n_pages, pt_ref, *refs):
    su_ref = refs[n_pages]
    o_ref = refs[n_pages + 1]
    x = jnp.concatenate([refs[g][0] for g in range(n_pages)], axis=0)
    later_in_page = (_iota((PAGE_SIZE, PAGE_SIZE), 0) > _iota((PAGE_SIZE, PAGE_SIZE), 1)).astype(F32)
    in_page = _dot(x, later_in_page, HIGHEST)
    totals = jnp.broadcast_to(jnp.sum(x, axis=1, keepdims=True), x.shape)
    later_pages = _dot(su_ref[...], totals, HIGHEST)
    o_ref[0] = (in_page + later_pages).reshape(n_pages, N_HEADS, PAGE_SIZE)


def _fox_suffix(cache_logf_t, pt_flat, b, pages_per_seq):
    rows = pages_per_seq * N_HEADS
    r = jnp.arange(rows)
    su = ((r[None, :] > r[:, None]) & (r[None, :] % N_HEADS == r[:, None] % N_HEADS)).astype(F32)
    grid_spec = pltpu.PrefetchScalarGridSpec(
        num_scalar_prefetch=1,
        grid=(b,),
        in_specs=[pl.BlockSpec((1, N_HEADS, PAGE_SIZE),
                               functools.partial(lambda g, bb, pt: (pt[bb * pages_per_seq + g], 0, 0), g))
                  for g in range(pages_per_seq)]
        + [pl.BlockSpec((rows, rows), lambda bb, pt: (0, 0))],
        out_specs=pl.BlockSpec((1, pages_per_seq, N_HEADS, PAGE_SIZE), lambda bb, pt: (bb, 0, 0, 0)),
    )
    return pl.pallas_call(
        functools.partial(_suffix_kernel, pages_per_seq),
        grid_spec=grid_spec,
        out_shape=jax.ShapeDtypeStruct((b, pages_per_seq, N_HEADS, PAGE_SIZE), F32),
        compiler_params=_cparams(("parallel",), 32),
        name="fox_sample_suffix",
    )(pt_flat, *([cache_logf_t] * pages_per_seq), su)


def _fox_sample_kernel(group, pt_ref, q_ref, kn_ref, vn_ref, lfn_ref, sfx_ref, *refs):
    k_refs = refs[:group]
    v_refs = refs[group:2 * group]
    o_ref = refs[2 * group]
    m_ref, l_ref, acc_ref = refs[2 * group + 1:]
    s = pl.program_id(1)
    t_new = q_ref.shape[1]
    qbd = _block_diag_rows(q_ref[0] * HEAD_DIM ** -0.5).astype(BF16)

    @pl.when(s == 0)
    def _():
        upto = (_iota((LANES, LANES), 0) <= _iota((LANES, LANES), 1)).astype(F32)
        pre_t = _dot(lfn_ref[0], upto, HIGHEST)
        _new_token_init(qbd, kn_ref, vn_ref, -jnp.concatenate([pre_t] * t_new, axis=0), m_ref, l_ref, acc_ref)

    kt = _pages(k_refs).astype(BF16)
    vt = _pages(v_refs).astype(BF16)
    bias = jnp.concatenate([jnp.concatenate([sfx_ref[0, g]] * t_new, axis=0) for g in range(group)], axis=1)
    sc = _dot(qbd, kt) + bias
    _online_update(sc, vt, m_ref, l_ref, acc_ref)

    @pl.when(s == pl.num_programs(1) - 1)
    def _():
        o_ref[0] = _collapse_rows(acc_ref[...] / l_ref[...])


def _fox_sample(q, k_new, v_new, lfn_t, sfx, cache_k, cache_v, pt_flat, pages_per_seq, group):
    b, t_new, _ = q.shape
    steps = pages_per_seq // group
    tok = pl.BlockSpec((1, t_new, WIDTH), lambda bb, s, pt: (bb, 0, 0))
    grid_spec = pltpu.PrefetchScalarGridSpec(
        num_scalar_prefetch=1,
        grid=(b, steps),
        in_specs=[tok, tok, tok,
                  pl.BlockSpec((1, N_HEADS, LANES), lambda bb, s, pt: (bb, 0, 0)),
                  pl.BlockSpec((1, group, N_HEADS, PAGE_SIZE), lambda bb, s, pt: (bb, s, 0, 0))]
        + [_page_spec(pages_per_seq, group, g) for g in range(group)] * 2,
        out_specs=tok,
        scratch_shapes=_sample_scratch(t_new),
    )
    return pl.pallas_call(
        functools.partial(_fox_sample_kernel, group),
        grid_spec=grid_spec,
        out_shape=jax.ShapeDtypeStruct((b, t_new, WIDTH), F32),
        compiler_params=_cparams(("parallel", "arbitrary"), 56),
        name="fox_sample_attn",
    )(pt_flat, q, k_new, v_new, lfn_t, sfx, *([cache_k] * group), *([cache_v] * group))


def _rope_tables(pos):
    half = HEAD_DIM // 2
    inv_freq = ROPE_THETA ** (-jnp.arange(half, dtype=F32) / half)
    ang = pos.astype(F32)[:, None] * inv_freq[None, :]
    cos = jnp.cos(ang)
    sin = jnp.sin(ang)
    cos_t = jnp.tile(cos, (1, LANES // half))
    sin_t = jnp.tile(jnp.concatenate([-sin, sin], axis=1), (1, LANES // HEAD_DIM))
    return cos_t, sin_t


def kernel(x_prompt, x_sample, cache_moba_k, cache_moba_v, cache_fox_k, cache_fox_v, cache_fox_logf, page_table, c_prompt, c_sample, norm_gain, w_ada, b_ada, w_in, b_forget, q_norm_a, k_norm_a, q_norm_b, k_norm_b, w_branch, w_out):
    n, t, d = x_prompt.shape
    b, t_new, _ = x_sample.shape
    depth = w_in.shape[0]
    n_pool = cache_moba_k.shape[1]
    pages = page_table.shape[1]
    past_len = pages * PAGE_SIZE
    assert t % MOBA_BLOCK == 0 and past_len % MOBA_BLOCK == 0 and (b * t_new) % ROW_TILE == 0
    assert ROW_TILE % t_new == 0 and t_new == N_HEADS
    group = min(16, pages)
    d_in = w_in.shape[2]
    w_cols = 12 * WIDTH + LANES

    pos_p = jnp.arange(t)
    pos_s = past_len + jnp.arange(t_new)
    cos_p, sin_p = _rope_tables(pos_p)
    cos_s, sin_s = (jnp.tile(a, (ROW_TILE // t_new, 1)) for a in _rope_tables(pos_s))
    lane_head = jnp.arange(WIDTH) // HEAD_DIM
    gsum = (lane_head[:, None] == lane_head[None, :]).astype(BF16)
    pt_flat = page_table.reshape(-1).astype(jnp.int32)

    yp = x_prompt.reshape(n * t, d)
    ys = x_sample.reshape(b * t_new, d)
    outs_p = [[] for _ in range(5)]
    outs_s = [[] for _ in range(5)]
    for l in range(depth):
        w_pad = jnp.pad(w_in[l], ((0, 0), (0, w_cols - d_in))).astype(BF16)
        bf_pad = jnp.pad(b_forget[l], (0, LANES - N_HEADS)).reshape(1, LANES)
        gains = [jnp.tile(g[l], N_HEADS).reshape(1, WIDTH) for g in (q_norm_a, k_norm_a, q_norm_b, k_norm_b)]
        gain = norm_gain[l].reshape(1, d)
        wb = w_branch[l].astype(BF16)
        wo = w_out[l].astype(BF16)

        mod = _ada_mod(c_prompt, w_ada[l], b_ada[l])
        shift, scale, gate = (mod[:, i * d:(i + 1) * d].reshape(n, 1, d) for i in range(3))
        (ka_t, va_t, kb_t, vb_t, logf_t, qa_s, qb_s, ka_b, kb_b, va_tb, vb_tb, za, zb, bg, selb, crep) = _proj_in(
            True, yp, shift, scale, gain, w_pad, bf_pad, *gains, cos_p, sin_p, gsum, n, t)
        as3 = lambda a: a.reshape(n, t, a.shape[-1])
        oa = _attn_prompt(False, as3(qa_s), as3(ka_b), va_tb, selb)
        ob = _attn_prompt(True, as3(qb_s), as3(kb_b), vb_tb, as3(crep))
        yp = _mixer_out(oa.reshape(n * t, WIDTH), ob.reshape(n * t, WIDTH), za, zb, bg, yp, gate, wb, wo,
                        t // ROW_TILE)
        for dst, a in zip(outs_p, (ka_t, va_t, kb_t, vb_t)):
            dst.append(a.reshape(n, N_HEADS, HEAD_DIM, t).transpose(0, 3, 1, 2))
        outs_p[4].append(logf_t.transpose(0, 2, 1))

        mod = _ada_mod(c_sample, w_ada[l], b_ada[l])
        rows = lambda a: jnp.repeat(a, t_new, axis=0).reshape(b * t_new // ROW_TILE, ROW_TILE, d)
        shift, scale, gate = (rows(mod[:, i * d:(i + 1) * d]) for i in range(3))
        (ka, va, kb, vb, logf, qa, qb, za, zb, bg) = _proj_in(
            False, ys, shift, scale, gain, w_pad, bf_pad, *gains, cos_s, sin_s, gsum, b, t_new)
        tok3 = lambda a: a.reshape(b, t_new, WIDTH)
        page_t = lambda c: c[l].transpose(0, 2, 3, 1).reshape(n_pool, WIDTH, PAGE_SIZE)
        ck_a, cv_a, ck_b, cv_b = (page_t(c) for c in (cache_moba_k, cache_moba_v, cache_fox_k, cache_fox_v))
        sel = _moba_select(tok3(qa), ck_a, pt_flat, pages, group)
        oa = _moba_sample(tok3(qa), tok3(ka), tok3(va), sel, ck_a, cv_a, pt_flat, pages, group)
        sfx = _fox_suffix(cache_fox_logf[l].transpose(0, 2, 1), pt_flat, b, pages)
        lfn_t = jnp.pad(logf.reshape(b, t_new, N_HEADS).transpose(0, 2, 1), ((0, 0), (0, 0), (0, LANES - t_new)))
        ob = _fox_sample(tok3(qb), tok3(kb), tok3(vb), lfn_t, sfx, ck_b, cv_b, pt_flat, pages, group)
        ys = _mixer_out(oa.reshape(b * t_new, WIDTH).astype(BF16), ob.reshape(b * t_new, WIDTH).astype(BF16),
                        za, zb, bg, ys, gate, wb, wo, 1)
        for dst, a in zip(outs_s, (ka, va, kb, vb)):
            dst.append(a.reshape(b, t_new, N_HEADS, HEAD_DIM))
        outs_s[4].append(logf.reshape(b, t_new, N_HEADS))

    return (yp.reshape(n, t, d), ys.reshape(b, t_new, d),
            *(jnp.stack(o) for o in outs_p), *(jnp.stack(o) for o in outs_s))
```

```python
import functools

import jax
import jax.numpy as jnp
from jax import lax
from jax.experimental import pallas as pl
from jax.experimental.pallas import tpu as pltpu

F32 = jnp.float32
BF16 = jnp.bfloat16
HIGHEST = lax.Precision.HIGHEST
NEG_INF = float("-inf")

HEAD_DIM = 64
N_HEADS = 8
WIDTH = N_HEADS * HEAD_DIM
N_BRANCH = 2
MOBA_BLOCK = 256
MOBA_TOPK = 3
PAGE_SIZE = 128
ROPE_THETA = 10000.0
EPS = 1e-6
LOG2E = 1.4426950408889634
LANES = 128
ROW_TILE = 256
ATTN_BLOCKS_PER_STEP = 4
MiB = 1024 * 1024


def _cparams(dims, vmem_mib):
    return pltpu.CompilerParams(dimension_semantics=dims, vmem_limit_bytes=vmem_mib * MiB)


def _dot(a, b, precision=None):
    return jnp.dot(a, b, preferred_element_type=F32, precision=precision)


def _dot_nt(a, b, precision=None):
    return lax.dot_general(a, b, (((1,), (1,)), ((), ())), preferred_element_type=F32, precision=precision)


def _iota(shape, dim):
    return lax.broadcasted_iota(jnp.int32, shape, dim)


def _ada_kernel(c_ref, w_ref, b_ref, o_ref):
    c = c_ref[...]
    a = c * jax.nn.sigmoid(c)
    o_ref[...] = _dot(a, w_ref[...], HIGHEST) + b_ref[...]


def _ada_mod(c, w_ada, b_ada):
    n, d = c.shape
    d3 = w_ada.shape[1]
    tn = 512
    return pl.pallas_call(
        _ada_kernel,
        grid=(d3 // tn,),
        in_specs=[pl.BlockSpec((n, d), lambda j: (0, 0)),
                  pl.BlockSpec((d, tn), lambda j: (0, j)),
                  pl.BlockSpec((1, tn), lambda j: (0, j))],
        out_specs=pl.BlockSpec((n, tn), lambda j: (0, j)),
        out_shape=jax.ShapeDtypeStruct((n, d3), F32),
        compiler_params=_cparams(("parallel",), 32),
        name="ada_mod",
    )(c, w_ada, b_ada.reshape(1, d3))


def _log_sigmoid(v):
    return jnp.minimum(v, 0.0) - jnp.log1p(jnp.exp(-jnp.abs(v)))


def _proj_kernel(prompt, tiles_per_seq, *refs):
    (x_ref, shift_ref, scale_ref, gain_ref, w_ref, bf_ref, gqa_ref, gka_ref, gqb_ref, gkb_ref,
     cos_ref, sin_ref, gsum_ref) = refs[:13]
    if prompt:
        (kat_ref, vat_ref, kbt_ref, vbt_ref, logft_ref, qa_ref, qb_ref, kab_ref, kbb_ref, vatb_ref, vbtb_ref,
         za_ref, zb_ref, bg_ref, selb_ref, crep_ref, cumt_ref, means_ref, carry_ref) = refs[13:]
    else:
        (ka_ref, va_ref, kb_ref, vb_ref, logf_ref, qa_ref, qb_ref, za_ref, zb_ref, bg_ref) = refs[13:]
    tm = x_ref.shape[0]
    it = pl.program_id(0) % tiles_per_seq

    x = x_ref[...]
    ms = jnp.mean(x * x, axis=-1, keepdims=True)
    h = x * lax.rsqrt(ms + EPS) * gain_ref[...]
    h = h * (1.0 + scale_ref[0]) + shift_ref[0]
    hb = h.astype(BF16)

    def seg(c0, width=WIDTH):
        return _dot(hb, w_ref[:, c0:c0 + width])

    def head_norm(z, g_ref):
        ss = _dot((z * z).astype(BF16), gsum_ref[...])
        return z * lax.rsqrt(ss * (1.0 / HEAD_DIM) + EPS) * g_ref[...]

    cosf = jnp.concatenate([cos_ref[...]] * (WIDTH // LANES), axis=1)
    sinf = jnp.concatenate([sin_ref[...]] * (WIDTH // LANES), axis=1)
    first_half = (_iota((tm, WIDTH), 1) & (HEAD_DIM - 1)) < (HEAD_DIM // 2)

    def rope(y):
        partner = jnp.where(first_half, pltpu.roll(y, WIDTH - HEAD_DIM // 2, 1), pltpu.roll(y, HEAD_DIM // 2, 1))
        return y * cosf + partner * sinf

    qa = rope(head_norm(seg(0), gqa_ref))
    ka = rope(head_norm(seg(WIDTH), gka_ref))
    va = seg(2 * WIDTH)
    za = seg(3 * WIDTH)
    za_ref[...] = (za * jax.nn.sigmoid(za)).astype(BF16)
    qb = head_norm(seg(4 * WIDTH), gqb_ref)
    kb = head_norm(seg(5 * WIDTH), gkb_ref)
    vb = seg(6 * WIDTH)
    zb = seg(7 * WIDTH)
    zb_ref[...] = (zb * jax.nn.sigmoid(zb)).astype(BF16)
    for s in range(2 * N_BRANCH):
        g = seg(8 * WIDTH + s * WIDTH)
        bg_ref[:, s * WIDTH:(s + 1) * WIDTH] = jax.nn.sigmoid(g).astype(BF16)
    lf = _log_sigmoid(seg(12 * WIDTH, LANES) + bf_ref[...])

    if not prompt:
        ka_ref[...] = ka
        va_ref[...] = va
        kb_ref[...] = kb
        vb_ref[...] = vb
        logf_ref[...] = lf[:, :N_HEADS]
        qa_ref[...] = qa
        qb_ref[...] = qb
        return

    scale = HEAD_DIM ** -0.5 * LOG2E
    qa_ref[...] = (qa * scale).astype(BF16)
    qb_ref[...] = (qb * scale).astype(BF16)
    kab_ref[...] = ka.astype(BF16)
    kbb_ref[...] = kb.astype(BF16)
    kat_ref[0] = ka.T
    kbt_ref[0] = kb.T
    va_t = va.T
    vb_t = vb.T
    vat_ref[0] = va_t
    vbt_ref[0] = vb_t
    vatb_ref[0] = va_t.astype(BF16)
    vbtb_ref[0] = vb_t.astype(BF16)
    logft_ref[0] = lf.T[:N_HEADS, :]

    nblk = means_ref.shape[0]

    @pl.when(it == 0)
    def _():
        means_ref[...] = jnp.zeros_like(means_ref)
        carry_ref[...] = jnp.zeros_like(carry_ref)

    tri = (_iota((tm, tm), 1) <= _iota((tm, tm), 0)).astype(F32)
    cum = _dot(tri, lf, HIGHEST) + carry_ref[...]
    carry_ref[...] = cum[tm - 1:tm, :]
    cum2 = cum * LOG2E
    cumt_ref[0] = cum2.T[:N_HEADS, :]
    for hh in range(N_HEADS):
        crep_ref[:, hh * LANES:(hh + 1) * LANES] = jnp.broadcast_to(cum2[:, hh:hh + 1], (tm, LANES))

    mt = means_ref[...]
    mbd = jnp.concatenate([mt] * N_HEADS, axis=0)
    rr = _iota(mbd.shape, 0)
    ll = _iota(mbd.shape, 1)
    head_of_row = jnp.zeros(mbd.shape, jnp.int32)
    for hh in range(1, N_HEADS):
        head_of_row = head_of_row + (rr >= hh * nblk).astype(jnp.int32)
    mbd = jnp.where(head_of_row == (ll >> 6), mbd, 0.0)
    st = _dot_nt(mbd, qa, HIGHEST)
    jj = _iota((nblk, tm), 0)
    jf = jj.astype(F32)
    valid = jj < it
    for hh in range(N_HEADS):
        s = jnp.where(valid, st[hh * nblk:(hh + 1) * nblk, :], NEG_INF)
        sel = jnp.zeros((nblk, tm), F32)
        for _ in range(MOBA_TOPK):
            m = jnp.max(s, axis=0, keepdims=True)
            idx = jnp.min(jnp.where(s == m, jf, float(nblk)), axis=0, keepdims=True)
            pick = jf == idx
            sel = jnp.where(pick & valid, 1.0, sel)
            s = jnp.where(pick, NEG_INF, s)
        selb_ref[0, 0, hh * nblk:(hh + 1) * nblk, :] = jnp.where(sel > 0.5, 0.0, NEG_INF)
    means_ref[pl.ds(it, 1), :] = jnp.mean(ka, axis=0, keepdims=True)


def _proj_in(prompt, x, shift, scale, gain, w_pad, bf_pad, gqa, gka, gqb, gkb, cos_t, sin_t, gsum, n_seq, seq_len):
    m, d = x.shape
    tm = ROW_TILE
    n_tiles = m // tm
    if prompt:
        tiles_per_seq = seq_len // tm
        tiles_per_mod = tiles_per_seq
    else:
        tiles_per_seq = 1
        tiles_per_mod = 1
    r = shift.shape[1]
    tab_tiles = cos_t.shape[0] // tm
    wp = w_pad.shape[1]
    row = lambda width: pl.BlockSpec((tm, width), lambda i: (i, 0))
    const = lambda shape: pl.BlockSpec(shape, lambda i: (0,) * len(shape))
    in_specs = [
        row(d),
        pl.BlockSpec((1, r, d), lambda i: (i // tiles_per_mod, 0, 0)),
        pl.BlockSpec((1, r, d), lambda i: (i // tiles_per_mod, 0, 0)),
        const((1, d)),
        const((d, wp)),
        const((1, LANES)),
        const((1, WIDTH)), const((1, WIDTH)), const((1, WIDTH)), const((1, WIDTH)),
        pl.BlockSpec((tm, LANES), lambda i: (i % tab_tiles, 0)),
        pl.BlockSpec((tm, LANES), lambda i: (i % tab_tiles, 0)),
        const((WIDTH, WIDTH)),
    ]
    f32_tok = jax.ShapeDtypeStruct((m, WIDTH), F32)
    bf_tok = jax.ShapeDtypeStruct((m, WIDTH), BF16)
    scratch = []
    if prompt:
        nblk = seq_len // MOBA_BLOCK
        vt = jax.ShapeDtypeStruct((n_seq, WIDTH, seq_len), BF16)
        ft = jax.ShapeDtypeStruct((n_seq, WIDTH, seq_len), F32)
        vt_spec = pl.BlockSpec((1, WIDTH, tm), lambda i: (i // tiles_per_seq, 0, i % tiles_per_seq))
        out_shape = [ft, ft, ft, ft, jax.ShapeDtypeStruct((n_seq, N_HEADS, seq_len), F32)]
        out_specs = [vt_spec, vt_spec, vt_spec, vt_spec,
                     pl.BlockSpec((1, N_HEADS, tm), lambda i: (i // tiles_per_seq, 0, i % tiles_per_seq))]
        out_shape += [bf_tok, bf_tok, bf_tok, bf_tok, vt, vt, bf_tok, bf_tok,
                      jax.ShapeDtypeStruct((m, 2 * N_BRANCH * WIDTH), BF16),
                      jax.ShapeDtypeStruct((n_seq, nblk, N_HEADS * nblk, tm), F32),
                      jax.ShapeDtypeStruct((m, N_HEADS * LANES), F32),
                      jax.ShapeDtypeStruct((n_seq, N_HEADS, seq_len), F32)]
        out_specs += [row(WIDTH), row(WIDTH), row(WIDTH), row(WIDTH), vt_spec, vt_spec, row(WIDTH), row(WIDTH),
                      row(2 * N_BRANCH * WIDTH),
                      pl.BlockSpec((1, 1, N_HEADS * nblk, tm), lambda i: (i // tiles_per_seq, i % tiles_per_seq, 0, 0)),
                      row(N_HEADS * LANES),
                      pl.BlockSpec((1, N_HEADS, tm), lambda i: (i // tiles_per_seq, 0, i % tiles_per_seq))]
        scratch = [pltpu.VMEM((nblk, WIDTH), F32), pltpu.VMEM((1, LANES), F32)]
    else:
        out_shape = [f32_tok, f32_tok, f32_tok, f32_tok, jax.ShapeDtypeStruct((m, N_HEADS), F32),
                     f32_tok, f32_tok, bf_tok, bf_tok, jax.ShapeDtypeStruct((m, 2 * N_BRANCH * WIDTH), BF16)]
        out_specs = [row(WIDTH), row(WIDTH), row(WIDTH), row(WIDTH), row(N_HEADS),
                     row(WIDTH), row(WIDTH), row(WIDTH), row(WIDTH), row(2 * N_BRANCH * WIDTH)]
    return pl.pallas_call(
        functools.partial(_proj_kernel, prompt, tiles_per_seq),
        grid=(n_tiles,),
        in_specs=in_specs,
        out_specs=out_specs,
        out_shape=out_shape,
        scratch_shapes=scratch,
        compiler_params=_cparams(("arbitrary",), 56),
        name="proj_in_prompt" if prompt else "proj_in_sample",
    )(x, shift, scale, gain, w_pad, bf_pad, gqa, gka, gqb, gkb, cos_t, sin_t, gsum)


def _attn_prompt_kernel(fox, online, bound_ref, q_ref, k_ref, vt_ref, b_ref, *rest):
    if fox:
        cq_ref, o_ref = rest[:2]
    else:
        o_ref = rest[0]
    scratch = rest[2:] if fox else rest[1:]
    hp = pl.program_id(1)
    i = pl.program_id(2)
    tq = q_ref.shape[1]
    bk = tq
    q = q_ref[0]
    lane = _iota(q.shape, 1)
    zero = jnp.zeros_like(q)
    qm = (jnp.where(lane < HEAD_DIM, q, zero), jnp.where(lane >= HEAD_DIM, q, zero))
    nblk = b_ref.shape[2] // 2 if not fox else 0
    bound = 0.0 if online else bound_ref[0, 0]
    if fox and not online:
        rq = [cq_ref[0, pl.ds(2 * hp + a, 1), :] - bound for a in range(2)]

    def exponents(a, j, diag):
        j0 = pl.multiple_of(j * bk, bk)
        s = _dot_nt(k_ref[0, pl.ds(j0, bk), :], qm[a])
        if fox:
            c = b_ref[0, pl.ds(j0, bk), a * LANES:(a + 1) * LANES]
            if not online:
                s = s + rq[a]
            s = s - jnp.concatenate([c] * (tq // LANES), axis=1)
        elif not diag:
            s = s + (b_ref[0, 0, pl.ds(a * nblk + j, 1), :] - bound)
        elif not online:
            s = s - bound
        if diag:
            s = jnp.where(_iota(s.shape, 0) <= _iota(s.shape, 1), s, NEG_INF)
        return s

    def values(a, j):
        j0 = pl.multiple_of(j * bk, bk)
        return vt_ref[0, a * HEAD_DIM:(a + 1) * HEAD_DIM, pl.ds(j0, bk)]

    if online:
        def update(state, j, diag):
            out = []
            for a in range(2):
                m, l, acc = state[a]
                e = exponents(a, j, diag)
                m_new = jnp.maximum(m, jnp.max(e, axis=0, keepdims=True))
                alpha = jnp.exp2(m - m_new)
                p = jnp.exp2(e - m_new)
                l = alpha * l + jnp.sum(p, axis=0, keepdims=True)
                acc = alpha * acc + _dot(values(a, j), p.astype(BF16))
                out.append((m_new, l, acc))
            return tuple(out)

        init = tuple((jnp.full((1, tq), NEG_INF, F32), jnp.zeros((1, tq), F32), jnp.zeros((HEAD_DIM, tq), F32))
                     for _ in range(2))
        state = update(init, i, True)
        state = lax.fori_loop(0, i, lambda j, st: update(st, j, False), state)
        outs = [acc / l for _, l, acc in state]
    else:
        l_ref, acc_ref = scratch

        def add_blocks(js, diag_last, first):
            es = [[exponents(a, j, diag_last and idx == len(js) - 1) for a in range(2)]
                  for idx, j in enumerate(js)]
            ps = [[jnp.exp2(e) for e in pair] for pair in es]
            for a in range(2):
                l_new = sum(jnp.sum(pair[a].reshape(bk // 8, 8, tq), axis=0) for pair in ps)
                acc_new = sum(_dot(values(a, j), pair[a].astype(BF16)) for j, pair in zip(js, ps))
                if first:
                    l_ref[a] = l_new
                    acc_ref[a] = acc_new
                else:
                    l_ref[a] += l_new
                    acc_ref[a] += acc_new

        u_blocks = ATTN_BLOCKS_PER_STEP
        rem = i % u_blocks
        for r in range(u_blocks):
            @pl.when(rem == r)
            def _():
                add_blocks([i - r + u for u in range(r)] + [i], True, True)

        def group(g, carry):
            add_blocks([g * u_blocks + u for u in range(u_blocks)], False, False)
            return carry

        lax.fori_loop(0, i // u_blocks, group, 0)
        outs = [acc_ref[a] / jnp.sum(l_ref[a], axis=0, keepdims=True) for a in range(2)]
    o_ref[0] = jnp.concatenate(outs, axis=0).T.astype(o_ref.dtype)


def _attn_prompt(fox, online, bound, q, k, vt, bias, cum_t=None):
    n, t, _ = q.shape
    tq = MOBA_BLOCK
    in_specs = [pl.BlockSpec(memory_space=pltpu.SMEM),
                pl.BlockSpec((1, tq, LANES), lambda b, hp, i: (b, i, hp)),
                pl.BlockSpec((1, t, LANES), lambda b, hp, i: (b, 0, hp)),
                pl.BlockSpec((1, LANES, t), lambda b, hp, i: (b, hp, 0))]
    args = [bound, q, k, vt, bias]
    if fox:
        in_specs += [pl.BlockSpec((1, t, 2 * LANES), lambda b, hp, i: (b, 0, hp)),
                     pl.BlockSpec((1, N_HEADS, tq), lambda b, hp, i: (b, 0, i))]
        args.append(cum_t)
    else:
        nblk = t // MOBA_BLOCK
        in_specs.append(pl.BlockSpec((1, 1, 2 * nblk, tq), lambda b, hp, i: (b, i, hp, 0)))
    name = ("fox" if fox else "moba") + "_prompt_attn" + ("_online" if online else "")
    return pl.pallas_call(
        functools.partial(_attn_prompt_kernel, fox, online),
        grid=(n, N_HEADS // 2, t // tq),
        in_specs=in_specs,
        out_specs=pl.BlockSpec((1, tq, LANES), lambda b, hp, i: (b, i, hp)),
        out_shape=jax.ShapeDtypeStruct((n, t, WIDTH), BF16),
        scratch_shapes=[] if online else [pltpu.VMEM((2, 8, tq), F32), pltpu.VMEM((2, HEAD_DIM, tq), F32)],
        compiler_params=_cparams(("parallel", "parallel", "arbitrary"), 56),
        name=name,
    )(*args)


def _attn_prompt_any_gain(fox, gq, gk, q, k, vt, bias, cum_t=None):
    slack = 1.02
    limit = 40.0
    bound = (HEAD_DIM ** 0.5) * jnp.max(jnp.abs(gq)) * jnp.max(jnp.abs(gk)) * slack
    bound2 = (bound * LOG2E).astype(F32).reshape(1, 1)
    args = (q, k, vt, bias) + ((cum_t,) if fox else ())
    return lax.cond(bound <= limit,
                    lambda *a: _attn_prompt(fox, False, bound2, *a),
                    lambda *a: _attn_prompt(fox, True, bound2, *a), *args)


def _out_kernel(oa_ref, ob_ref, za_ref, zb_ref, bg_ref, x_ref, gate_ref, wb_ref, wo_ref, y_ref):
    d = x_ref.shape[1]
    ga = (oa_ref[...].astype(F32) * za_ref[...].astype(F32)).astype(BF16)
    gb = (ob_ref[...].astype(F32) * zb_ref[...].astype(F32)).astype(BF16)
    ua = _dot(ga, wb_ref[0])
    ub = _dot(gb, wb_ref[1])
    merged = bg_ref[:, :d].astype(F32) * ua + bg_ref[:, d:].astype(F32) * ub
    y_ref[...] = x_ref[...] + gate_ref[0] * _dot(merged.astype(BF16), wo_ref[...])


def _mixer_out(oa, ob, za, zb, bg, x, gate, wb, wo, tiles_per_mod):
    m, d = x.shape
    tm = ROW_TILE
    r = gate.shape[1]
    row = lambda width: pl.BlockSpec((tm, width), lambda i: (i, 0))
    return pl.pallas_call(
        _out_kernel,
        grid=(m // tm,),
        in_specs=[row(WIDTH), row(WIDTH), row(WIDTH), row(WIDTH), row(N_BRANCH * d), row(d),
                  pl.BlockSpec((1, r, d), lambda i: (i // tiles_per_mod, 0, 0)),
                  pl.BlockSpec((N_BRANCH, WIDTH, d), lambda i: (0, 0, 0)),
                  pl.BlockSpec((d, d), lambda i: (0, 0))],
        out_specs=row(d),
        out_shape=jax.ShapeDtypeStruct((m, d), F32),
        compiler_params=_cparams(("parallel",), 48),
        name="mixer_out",
    )(oa, ob, za, zb, bg, x, gate, wb, wo)


def _block_diag_rows(q):
    t_new = q.shape[0]
    keep = (_iota((N_HEADS, WIDTH), 1) >> 6) == _iota((N_HEADS, WIDTH), 0)
    return jnp.concatenate([jnp.where(keep, q[t:t + 1, :], 0.0) for t in range(t_new)], axis=0)


def _collapse_rows(o64):
    rows = o64.shape[0]
    keep = (_iota((rows, WIDTH), 1) >> 6) == (_iota((rows, WIDTH), 0) & (N_HEADS - 1))
    om = jnp.where(keep, o64, 0.0)
    return jnp.concatenate([jnp.sum(om[t * N_HEADS:(t + 1) * N_HEADS], axis=0, keepdims=True)
                            for t in range(rows // N_HEADS)], axis=0)


def _pad_rows(a, rows):
    return jnp.concatenate([a, jnp.zeros((rows - a.shape[0], a.shape[1]), a.dtype)], axis=0)


def _new_token_init(qbd, kn_ref, vn_ref, bias, m_ref, l_ref, acc_ref):
    kn = _pad_rows(kn_ref[0], LANES).astype(BF16)
    vn = _pad_rows(vn_ref[0], LANES).astype(BF16)
    s = _dot_nt(qbd, kn)
    if bias is not None:
        s = s + bias
    s = jnp.where(_iota(s.shape, 1) <= (_iota(s.shape, 0) >> 3), s, NEG_INF)
    m = jnp.max(s, axis=1, keepdims=True)
    p = jnp.exp(s - m)
    m_ref[...] = m
    l_ref[...] = jnp.sum(p, axis=1, keepdims=True)
    acc_ref[...] = _dot(p.astype(BF16), vn)


def _online_update(s, vt, m_ref, l_ref, acc_ref):
    m_old = m_ref[...]
    m_new = jnp.maximum(m_old, jnp.max(s, axis=1, keepdims=True))
    alpha = jnp.exp(m_old - m_new)
    p = jnp.exp(s - m_new)
    m_ref[...] = m_new
    l_ref[...] = alpha * l_ref[...] + jnp.sum(p, axis=1, keepdims=True)
    acc_ref[...] = alpha * acc_ref[...] + _dot_nt(p.astype(BF16), vt)


def _page_spec(pages_per_seq, group, g):
    return pl.BlockSpec((1, WIDTH, PAGE_SIZE),
                        lambda b, s, pt: (pt[b * pages_per_seq + s * group + g], 0, 0))


def _pages(page_refs):
    return jnp.concatenate([r[0] for r in page_refs], axis=1)


def _moba_sample_kernel(group, steps, n_blocks, pt_ref, q_ref, kn_ref, vn_ref, *refs):
    k_refs = refs[:group]
    v_refs = refs[group:2 * group]
    o_ref = refs[2 * group]
    bm_ref, sc_ref, sel_ref, m_ref, l_ref, acc_ref = refs[2 * group + 1:]
    s = pl.program_id(1)
    ppb = MOBA_BLOCK // PAGE_SIZE
    keys = group * PAGE_SIZE
    q = q_ref[0]

    @pl.when(s == 0)
    def _():
        bm_ref[...] = jnp.zeros_like(bm_ref)

    @pl.when(s < steps)
    def _():
        lane_blk = _iota(bm_ref.shape, 1)
        for jb in range(group // ppb):
            tot = k_refs[ppb * jb][0]
            for e in range(1, ppb):
                tot = tot + k_refs[ppb * jb + e][0]
            mean = jnp.sum(tot, axis=1, keepdims=True) * (1.0 / MOBA_BLOCK)
            bm_ref[...] = jnp.where(lane_blk == s * (group // ppb) + jb, mean, bm_ref[...])
        qbd = _block_diag_rows(q * HEAD_DIM ** -0.5).astype(BF16)
        k0 = pl.multiple_of(s * keys, keys)
        sc_ref[:, pl.ds(k0, keys)] = _dot(qbd, _pages(k_refs).astype(BF16))

    @pl.when(s == steps - 1)
    def _():
        g = _dot(_block_diag_rows(q), bm_ref[...], HIGHEST)
        lane = _iota(g.shape, 1)
        lf = lane.astype(F32)
        valid = lane < n_blocks
        g = jnp.where(valid, g, NEG_INF)
        sel = jnp.zeros(g.shape, F32)
        for _ in range(min(MOBA_TOPK, n_blocks)):
            m = jnp.max(g, axis=1, keepdims=True)
            idx = jnp.min(jnp.where(g == m, lf, float(LANES)), axis=1, keepdims=True)
            pick = lf == idx
            sel = jnp.where(pick & valid, 1.0, sel)
            g = jnp.where(pick, NEG_INF, g)
        sel_ref[...] = sel
        qbd = _block_diag_rows(q * HEAD_DIM ** -0.5).astype(BF16)
        _new_token_init(qbd, kn_ref, vn_ref, None, m_ref, l_ref, acc_ref)

    @pl.when(s >= steps)
    def _():
        sv = s - steps
        k0 = pl.multiple_of(sv * keys, keys)
        blk_of_key = sv * (keys // MOBA_BLOCK) + (_iota((LANES, keys), 1) >> 8)
        expand = (_iota((LANES, keys), 0) == blk_of_key).astype(BF16)
        chosen = _dot(sel_ref[...].astype(BF16), expand)
        sc = jnp.where(chosen > 0.5, sc_ref[:, pl.ds(k0, keys)], NEG_INF)
        _online_update(sc, _pages(v_refs).astype(BF16), m_ref, l_ref, acc_ref)

    @pl.when(s == 2 * steps - 1)
    def _():
        o_ref[0] = _collapse_rows(acc_ref[...] / l_ref[...])


def _sample_scratch(t_new):
    rows = t_new * N_HEADS
    return [pltpu.VMEM((rows, 1), F32), pltpu.VMEM((rows, 1), F32), pltpu.VMEM((rows, WIDTH), F32)]


def _moba_sample(q, k_new, v_new, cache_k, cache_v, pt_flat, pages_per_seq, group):
    b, t_new, _ = q.shape
    steps = pages_per_seq // group
    n_blocks = pages_per_seq * PAGE_SIZE // MOBA_BLOCK
    rows = t_new * N_HEADS
    tok = pl.BlockSpec((1, t_new, WIDTH), lambda bb, s, pt: (bb, 0, 0))

    def page(g, phase):
        def index(bb, s, pt):
            step = jnp.minimum(s, steps - 1) if phase == 0 else jnp.maximum(s - steps, 0)
            return (pt[bb * pages_per_seq + step * group + g], 0, 0)
        return pl.BlockSpec((1, WIDTH, PAGE_SIZE), index)

    grid_spec = pltpu.PrefetchScalarGridSpec(
        num_scalar_prefetch=1,
        grid=(b, 2 * steps),
        in_specs=[tok, tok, tok] + [page(g, 0) for g in range(group)] + [page(g, 1) for g in range(group)],
        out_specs=tok,
        scratch_shapes=[pltpu.VMEM((WIDTH, LANES), F32),
                        pltpu.VMEM((rows, pages_per_seq * PAGE_SIZE), F32),
                        pltpu.VMEM((rows, LANES), F32)] + _sample_scratch(t_new),
    )
    return pl.pallas_call(
        functools.partial(_moba_sample_kernel, group, steps, n_blocks),
        grid_spec=grid_spec,
        out_shape=jax.ShapeDtypeStruct((b, t_new, WIDTH), F32),
        compiler_params=_cparams(("parallel", "arbitrary"), 56),
        name="moba_sample_attn",
    )(pt_flat, q, k_new, v_new, *([cache_k] * group), *([cache_v] * group))


def _suffix_kernel(n_pages, pt_ref, *refs):
    su_ref = refs[n_pages]
    o_ref = refs[n_pages + 1]
    x = jnp.concatenate([refs[g][0] for g in range(n_pages)], axis=0)
    later_in_page = (_iota((PAGE_SIZE, PAGE_SIZE), 0) > _iota((PAGE_SIZE, PAGE_SIZE), 1)).astype(F32)
    in_page = _dot(x, later_in_page, HIGHEST)
    totals = jnp.broadcast_to(jnp.sum(x, axis=1, keepdims=True), x.shape)
    later_pages = _dot(su_ref[...], totals, HIGHEST)
    o_ref[0] = (in_page + later_pages).reshape(n_pages, N_HEADS, PAGE_SIZE)


def _fox_suffix(cache_logf_t, pt_flat, b, pages_per_seq):
    rows = pages_per_seq * N_HEADS
    r = jnp.arange(rows)
    su = ((r[None, :] > r[:, None]) & (r[None, :] % N_HEADS == r[:, None] % N_HEADS)).astype(F32)
    grid_spec = pltpu.PrefetchScalarGridSpec(
        num_scalar_prefetch=1,
        grid=(b,),
        in_specs=[pl.BlockSpec((1, N_HEADS, PAGE_SIZE),
                               functools.partial(lambda g, bb, pt: (pt[bb * pages_per_seq + g], 0, 0), g))
                  for g in range(pages_per_seq)]
        + [pl.BlockSpec((rows, rows), lambda bb, pt: (0, 0))],
        out_specs=pl.BlockSpec((1, pages_per_seq, N_HEADS, PAGE_SIZE), lambda bb, pt: (bb, 0, 0, 0)),
    )
    return pl.pallas_call(
        functools.partial(_suffix_kernel, pages_per_seq),
        grid_spec=grid_spec,
        out_shape=jax.ShapeDtypeStruct((b, pages_per_seq, N_HEADS, PAGE_SIZE), F32),
        compiler_params=_cparams(("parallel",), 32),
        name="fox_sample_suffix",
    )(pt_flat, *([cache_logf_t] * pages_per_seq), su)


def _fox_sample_kernel(group, pt_ref, q_ref, kn_ref, vn_ref, lfn_ref, sfx_ref, *refs):
    k_refs = refs[:group]
    v_refs = refs[group:2 * group]
    o_ref = refs[2 * group]
    m_ref, l_ref, acc_ref = refs[2 * group + 1:]
    s = pl.program_id(1)
    t_new = q_ref.shape[1]
    qbd = _block_diag_rows(q_ref[0] * HEAD_DIM ** -0.5).astype(BF16)

    @pl.when(s == 0)
    def _():
        upto = (_iota((LANES, LANES), 0) <= _iota((LANES, LANES), 1)).astype(F32)
        pre_t = _dot(lfn_ref[0], upto, HIGHEST)
        _new_token_init(qbd, kn_ref, vn_ref, -jnp.concatenate([pre_t] * t_new, axis=0), m_ref, l_ref, acc_ref)

    kt = _pages(k_refs).astype(BF16)
    vt = _pages(v_refs).astype(BF16)
    bias = jnp.concatenate([jnp.concatenate([sfx_ref[0, g]] * t_new, axis=0) for g in range(group)], axis=1)
    sc = _dot(qbd, kt) + bias
    _online_update(sc, vt, m_ref, l_ref, acc_ref)

    @pl.when(s == pl.num_programs(1) - 1)
    def _():
        o_ref[0] = _collapse_rows(acc_ref[...] / l_ref[...])


def _fox_sample(q, k_new, v_new, lfn_t, sfx, cache_k, cache_v, pt_flat, pages_per_seq, group):
    b, t_new, _ = q.shape
    steps = pages_per_seq // group
    tok = pl.BlockSpec((1, t_new, WIDTH), lambda bb, s, pt: (bb, 0, 0))
    grid_spec = pltpu.PrefetchScalarGridSpec(
        num_scalar_prefetch=1,
        grid=(b, steps),
        in_specs=[tok, tok, tok,
                  pl.BlockSpec((1, N_HEADS, LANES), lambda bb, s, pt: (bb, 0, 0)),
                  pl.BlockSpec((1, group, N_HEADS, PAGE_SIZE), lambda bb, s, pt: (bb, s, 0, 0))]
        + [_page_spec(pages_per_seq, group, g) for g in range(group)] * 2,
        out_specs=tok,
        scratch_shapes=_sample_scratch(t_new),
    )
    return pl.pallas_call(
        functools.partial(_fox_sample_kernel, group),
        grid_spec=grid_spec,
        out_shape=jax.ShapeDtypeStruct((b, t_new, WIDTH), F32),
        compiler_params=_cparams(("parallel", "arbitrary"), 56),
        name="fox_sample_attn",
    )(pt_flat, q, k_new, v_new, lfn_t, sfx, *([cache_k] * group), *([cache_v] * group))


def _rope_tables(pos):
    half = HEAD_DIM // 2
    inv_freq = ROPE_THETA ** (-jnp.arange(half, dtype=F32) / half)
    ang = pos.astype(F32)[:, None] * inv_freq[None, :]
    cos = jnp.cos(ang)
    sin = jnp.sin(ang)
    cos_t = jnp.tile(cos, (1, LANES // half))
    sin_t = jnp.tile(jnp.concatenate([-sin, sin], axis=1), (1, LANES // HEAD_DIM))
    return cos_t, sin_t


def kernel(x_prompt, x_sample, cache_moba_k, cache_moba_v, cache_fox_k, cache_fox_v, cache_fox_logf, page_table, c_prompt, c_sample, norm_gain, w_ada, b_ada, w_in, b_forget, q_norm_a, k_norm_a, q_norm_b, k_norm_b, w_branch, w_out):
    n, t, d = x_prompt.shape
    b, t_new, _ = x_sample.shape
    depth = w_in.shape[0]
    n_pool = cache_moba_k.shape[1]
    pages = page_table.shape[1]
    past_len = pages * PAGE_SIZE
    assert t % MOBA_BLOCK == 0 and past_len % MOBA_BLOCK == 0 and (b * t_new) % ROW_TILE == 0
    assert ROW_TILE % t_new == 0 and t_new == N_HEADS
    group = min(16, pages)
    d_in = w_in.shape[2]
    w_cols = 12 * WIDTH + LANES

    pos_p = jnp.arange(t)
    pos_s = past_len + jnp.arange(t_new)
    cos_p, sin_p = _rope_tables(pos_p)
    cos_s, sin_s = (jnp.tile(a, (ROW_TILE // t_new, 1)) for a in _rope_tables(pos_s))
    lane_head = jnp.arange(WIDTH) // HEAD_DIM
    gsum = (lane_head[:, None] == lane_head[None, :]).astype(BF16)
    pt_flat = page_table.reshape(-1).astype(jnp.int32)

    yp = x_prompt.reshape(n * t, d)
    ys = x_sample.reshape(b * t_new, d)
    outs_p = [[] for _ in range(5)]
    outs_s = [[] for _ in range(5)]
    for l in range(depth):
        w_pad = jnp.pad(w_in[l], ((0, 0), (0, w_cols - d_in))).astype(BF16)
        bf_pad = jnp.pad(b_forget[l], (0, LANES - N_HEADS)).reshape(1, LANES)
        gains = [jnp.tile(g[l], N_HEADS).reshape(1, WIDTH) for g in (q_norm_a, k_norm_a, q_norm_b, k_norm_b)]
        gain = norm_gain[l].reshape(1, d)
        wb = w_branch[l].astype(BF16)
        wo = w_out[l].astype(BF16)

        mod = _ada_mod(c_prompt, w_ada[l], b_ada[l])
        shift, scale, gate = (mod[:, i * d:(i + 1) * d].reshape(n, 1, d) for i in range(3))
        (ka_t, va_t, kb_t, vb_t, logf_t, qa_s, qb_s, ka_b, kb_b, va_tb, vb_tb, za, zb, bg, selb, crep,
         cum_t) = _proj_in(True, yp, shift, scale, gain, w_pad, bf_pad, *gains, cos_p, sin_p, gsum, n, t)
        as3 = lambda a: a.reshape(n, t, a.shape[-1])
        oa = _attn_prompt_any_gain(False, q_norm_a[l], k_norm_a[l], as3(qa_s), as3(ka_b), va_tb, selb)
        ob = _attn_prompt_any_gain(True, q_norm_b[l], k_norm_b[l], as3(qb_s), as3(kb_b), vb_tb, as3(crep), cum_t)
        yp = _mixer_out(oa.reshape(n * t, WIDTH), ob.reshape(n * t, WIDTH), za, zb, bg, yp, gate, wb, wo,
                        t // ROW_TILE)
        for dst, a in zip(outs_p, (ka_t, va_t, kb_t, vb_t)):
            dst.append(a.reshape(n, N_HEADS, HEAD_DIM, t).transpose(0, 3, 1, 2))
        outs_p[4].append(logf_t.transpose(0, 2, 1))

        mod = _ada_mod(c_sample, w_ada[l], b_ada[l])
        rows = lambda a: jnp.repeat(a, t_new, axis=0).reshape(b * t_new // ROW_TILE, ROW_TILE, d)
        shift, scale, gate = (rows(mod[:, i * d:(i + 1) * d]) for i in range(3))
        (ka, va, kb, vb, logf, qa, qb, za, zb, bg) = _proj_in(
            False, ys, shift, scale, gain, w_pad, bf_pad, *gains, cos_s, sin_s, gsum, b, t_new)
        tok3 = lambda a: a.reshape(b, t_new, WIDTH)
        page_t = lambda c: c[l].transpose(0, 2, 3, 1).reshape(n_pool, WIDTH, PAGE_SIZE)
        ck_a, cv_a, ck_b, cv_b = (page_t(c) for c in (cache_moba_k, cache_moba_v, cache_fox_k, cache_fox_v))
        oa = _moba_sample(tok3(qa), tok3(ka), tok3(va), ck_a, cv_a, pt_flat, pages, group)
        sfx = _fox_suffix(cache_fox_logf[l].transpose(0, 2, 1), pt_flat, b, pages)
        lfn_t = jnp.pad(logf.reshape(b, t_new, N_HEADS).transpose(0, 2, 1), ((0, 0), (0, 0), (0, LANES - t_new)))
        ob = _fox_sample(tok3(qb), tok3(kb), tok3(vb), lfn_t, sfx, ck_b, cv_b, pt_flat, pages, group)
        ys = _mixer_out(oa.reshape(b * t_new, WIDTH).astype(BF16), ob.reshape(b * t_new, WIDTH).astype(BF16),
                        za, zb, bg, ys, gate, wb, wo, 1)
        for dst, a in zip(outs_s, (ka, va, kb, vb)):
            dst.append(a.reshape(b, t_new, N_HEADS, HEAD_DIM))
        outs_s[4].append(logf.reshape(b, t_new, N_HEADS))

    return (yp.reshape(n, t, d), ys.reshape(b, t_new, d),
            *(jnp.stack(o) for o in outs_p), *(jnp.stack(o) for o in outs_s))
```

```python
import functools

import jax
import jax.numpy as jnp
from jax import lax
from jax.experimental import pallas as pl
from jax.experimental.pallas import tpu as pltpu

F32 = jnp.float32
BF16 = jnp.bfloat16
HIGHEST = lax.Precision.HIGHEST
NEG_INF = float("-inf")

HEAD_DIM = 64
N_HEADS = 8
WIDTH = N_HEADS * HEAD_DIM
N_BRANCH = 2
MOBA_BLOCK = 256
MOBA_TOPK = 3
PAGE_SIZE = 128
ROPE_THETA = 10000.0
EPS = 1e-6
LOG2E = 1.4426950408889634
LANES = 128
ROW_TILE = 256
ATTN_BLOCKS_PER_STEP = 4
MiB = 1024 * 1024


def _cparams(dims, vmem_mib):
    return pltpu.CompilerParams(dimension_semantics=dims, vmem_limit_bytes=vmem_mib * MiB)


def _dot(a, b, precision=None):
    return jnp.dot(a, b, preferred_element_type=F32, precision=precision)


def _dot_nt(a, b, precision=None):
    return lax.dot_general(a, b, (((1,), (1,)), ((), ())), preferred_element_type=F32, precision=precision)


def _iota(shape, dim):
    return lax.broadcasted_iota(jnp.int32, shape, dim)


def _split_bf16(x, terms):
    parts = []
    for _ in range(terms):
        p = x.astype(BF16)
        parts.append(p)
        x = x - p.astype(F32)
    return parts


def _ada_kernel(c_ref, w_ref, b_ref, o_ref):
    c = c_ref[...]
    a = c * jax.nn.sigmoid(c)
    o_ref[...] = _dot(a, w_ref[...], HIGHEST) + b_ref[...]


def _ada_mod(c, w_ada, b_ada):
    n, d = c.shape
    d3 = w_ada.shape[1]
    tn = 512
    return pl.pallas_call(
        _ada_kernel,
        grid=(d3 // tn,),
        in_specs=[pl.BlockSpec((n, d), lambda j: (0, 0)),
                  pl.BlockSpec((d, tn), lambda j: (0, j)),
                  pl.BlockSpec((1, tn), lambda j: (0, j))],
        out_specs=pl.BlockSpec((n, tn), lambda j: (0, j)),
        out_shape=jax.ShapeDtypeStruct((n, d3), F32),
        compiler_params=_cparams(("parallel",), 32),
        name="ada_mod",
    )(c, w_ada, b_ada.reshape(1, d3))


def _log_sigmoid(v):
    return jnp.minimum(v, 0.0) - jnp.log1p(jnp.exp(-jnp.abs(v)))


def _proj_kernel(prompt, tiles_per_seq, *refs):
    (x_ref, shift_ref, scale_ref, gain_ref, w_ref, bf_ref, gqa_ref, gka_ref, gqb_ref, gkb_ref,
     cos_ref, sin_ref, gsum_ref) = refs[:13]
    if prompt:
        (kat_ref, vat_ref, kbt_ref, vbt_ref, logft_ref, qa_ref, qb_ref, kab_ref, kbb_ref, vatb_ref, vbtb_ref,
         za_ref, zb_ref, bg_ref, selb_ref, crep_ref, cumt_ref, means_ref, carry_ref) = refs[13:]
    else:
        (ka_ref, va_ref, kb_ref, vb_ref, logf_ref, qa_ref, qb_ref, za_ref, zb_ref, bg_ref) = refs[13:]
    tm = x_ref.shape[0]
    it = pl.program_id(0) % tiles_per_seq

    x = x_ref[...]
    ms = jnp.mean(x * x, axis=-1, keepdims=True)
    h = x * lax.rsqrt(ms + EPS) * gain_ref[...]
    h = h * (1.0 + scale_ref[0]) + shift_ref[0]
    hb = h.astype(BF16)

    def seg(c0, width=WIDTH):
        return _dot(hb, w_ref[:, c0:c0 + width])

    def head_norm(z, g_ref):
        ss = _dot((z * z).astype(BF16), gsum_ref[...])
        return z * lax.rsqrt(ss * (1.0 / HEAD_DIM) + EPS) * g_ref[...]

    cosf = jnp.concatenate([cos_ref[...]] * (WIDTH // LANES), axis=1)
    sinf = jnp.concatenate([sin_ref[...]] * (WIDTH // LANES), axis=1)
    first_half = (_iota((tm, WIDTH), 1) & (HEAD_DIM - 1)) < (HEAD_DIM // 2)

    def rope(y):
        partner = jnp.where(first_half, pltpu.roll(y, WIDTH - HEAD_DIM // 2, 1), pltpu.roll(y, HEAD_DIM // 2, 1))
        return y * cosf + partner * sinf

    qa = rope(head_norm(seg(0), gqa_ref))
    ka = rope(head_norm(seg(WIDTH), gka_ref))
    va = seg(2 * WIDTH)
    za = seg(3 * WIDTH)
    za_ref[...] = (za * jax.nn.sigmoid(za)).astype(BF16)
    qb = head_norm(seg(4 * WIDTH), gqb_ref)
    kb = head_norm(seg(5 * WIDTH), gkb_ref)
    vb = seg(6 * WIDTH)
    zb = seg(7 * WIDTH)
    zb_ref[...] = (zb * jax.nn.sigmoid(zb)).astype(BF16)
    for s in range(2 * N_BRANCH):
        g = seg(8 * WIDTH + s * WIDTH)
        bg_ref[:, s * WIDTH:(s + 1) * WIDTH] = jax.nn.sigmoid(g).astype(BF16)
    lf = _log_sigmoid(seg(12 * WIDTH, LANES) + bf_ref[...])

    if not prompt:
        ka_ref[...] = ka
        va_ref[...] = va
        kb_ref[...] = kb
        vb_ref[...] = vb
        logf_ref[...] = lf[:, :N_HEADS]
        qa_ref[...] = qa
        qb_ref[...] = qb
        return

    scale = HEAD_DIM ** -0.5 * LOG2E
    qa_ref[...] = (qa * scale).astype(BF16)
    qb_ref[...] = (qb * scale).astype(BF16)
    kab_ref[...] = ka.astype(BF16)
    kbb_ref[...] = kb.astype(BF16)
    kat_ref[0] = ka.T
    kbt_ref[0] = kb.T
    va_t = va.T
    vb_t = vb.T
    vat_ref[0] = va_t
    vbt_ref[0] = vb_t
    vatb_ref[0] = va_t.astype(BF16)
    vbtb_ref[0] = vb_t.astype(BF16)
    logft_ref[0] = lf.T[:N_HEADS, :]

    nblk = means_ref.shape[0]

    @pl.when(it == 0)
    def _():
        means_ref[...] = jnp.zeros_like(means_ref)
        carry_ref[...] = jnp.zeros_like(carry_ref)

    tri = (_iota((tm, tm), 1) <= _iota((tm, tm), 0)).astype(BF16)
    cum = sum(_dot(tri, part) for part in _split_bf16(lf, 3)) + carry_ref[...]
    carry_ref[...] = cum[tm - 1:tm, :]
    cum2 = cum * LOG2E
    cumt_ref[0] = cum2.T[:N_HEADS, :]
    for hh in range(N_HEADS):
        crep_ref[:, hh * LANES:(hh + 1) * LANES] = jnp.broadcast_to(cum2[:, hh:hh + 1], (tm, LANES))

    mt = means_ref[...]
    mbd = jnp.concatenate([mt] * N_HEADS, axis=0)
    rr = _iota(mbd.shape, 0)
    ll = _iota(mbd.shape, 1)
    head_of_row = jnp.zeros(mbd.shape, jnp.int32)
    for hh in range(1, N_HEADS):
        head_of_row = head_of_row + (rr >= hh * nblk).astype(jnp.int32)
    mbd = jnp.where(head_of_row == (ll >> 6), mbd, 0.0)
    m_hi, m_lo = _split_bf16(mbd, 2)
    q_hi, q_lo = _split_bf16(qa, 2)
    st = _dot_nt(m_hi, q_hi) + (_dot_nt(m_hi, q_lo) + _dot_nt(m_lo, q_hi))
    jj = _iota((nblk, tm), 0)
    jf = jj.astype(F32)
    valid = jj < it
    for hh in range(N_HEADS):
        s = jnp.where(valid, st[hh * nblk:(hh + 1) * nblk, :], NEG_INF)
        sel = jnp.zeros((nblk, tm), F32)
        for _ in range(MOBA_TOPK):
            m = jnp.max(s, axis=0, keepdims=True)
            idx = jnp.min(jnp.where(s == m, jf, float(nblk)), axis=0, keepdims=True)
            pick = jf == idx
            sel = jnp.where(pick & valid, 1.0, sel)
            s = jnp.where(pick, NEG_INF, s)
        selb_ref[0, 0, hh * nblk:(hh + 1) * nblk, :] = jnp.where(sel > 0.5, 0.0, NEG_INF)
    means_ref[pl.ds(it, 1), :] = jnp.mean(ka, axis=0, keepdims=True)


def _proj_in(prompt, x, shift, scale, gain, w_pad, bf_pad, gqa, gka, gqb, gkb, cos_t, sin_t, gsum, n_seq, seq_len):
    m, d = x.shape
    tm = ROW_TILE
    n_tiles = m // tm
    if prompt:
        tiles_per_seq = seq_len // tm
        tiles_per_mod = tiles_per_seq
    else:
        tiles_per_seq = 1
        tiles_per_mod = 1
    r = shift.shape[1]
    tab_tiles = cos_t.shape[0] // tm
    wp = w_pad.shape[1]
    row = lambda width: pl.BlockSpec((tm, width), lambda i: (i, 0))
    const = lambda shape: pl.BlockSpec(shape, lambda i: (0,) * len(shape))
    in_specs = [
        row(d),
        pl.BlockSpec((1, r, d), lambda i: (i // tiles_per_mod, 0, 0)),
        pl.BlockSpec((1, r, d), lambda i: (i // tiles_per_mod, 0, 0)),
        const((1, d)),
        const((d, wp)),
        const((1, LANES)),
        const((1, WIDTH)), const((1, WIDTH)), const((1, WIDTH)), const((1, WIDTH)),
        pl.BlockSpec((tm, LANES), lambda i: (i % tab_tiles, 0)),
        pl.BlockSpec((tm, LANES), lambda i: (i % tab_tiles, 0)),
        const((WIDTH, WIDTH)),
    ]
    f32_tok = jax.ShapeDtypeStruct((m, WIDTH), F32)
    bf_tok = jax.ShapeDtypeStruct((m, WIDTH), BF16)
    scratch = []
    if prompt:
        nblk = seq_len // MOBA_BLOCK
        vt = jax.ShapeDtypeStruct((n_seq, WIDTH, seq_len), BF16)
        ft = jax.ShapeDtypeStruct((n_seq, WIDTH, seq_len), F32)
        vt_spec = pl.BlockSpec((1, WIDTH, tm), lambda i: (i // tiles_per_seq, 0, i % tiles_per_seq))
        out_shape = [ft, ft, ft, ft, jax.ShapeDtypeStruct((n_seq, N_HEADS, seq_len), F32)]
        out_specs = [vt_spec, vt_spec, vt_spec, vt_spec,
                     pl.BlockSpec((1, N_HEADS, tm), lambda i: (i // tiles_per_seq, 0, i % tiles_per_seq))]
        out_shape += [bf_tok, bf_tok, bf_tok, bf_tok, vt, vt, bf_tok, bf_tok,
                      jax.ShapeDtypeStruct((m, 2 * N_BRANCH * WIDTH), BF16),
                      jax.ShapeDtypeStruct((n_seq, nblk, N_HEADS * nblk, tm), F32),
                      jax.ShapeDtypeStruct((m, N_HEADS * LANES), F32),
                      jax.ShapeDtypeStruct((n_seq, N_HEADS, seq_len), F32)]
        out_specs += [row(WIDTH), row(WIDTH), row(WIDTH), row(WIDTH), vt_spec, vt_spec, row(WIDTH), row(WIDTH),
                      row(2 * N_BRANCH * WIDTH),
                      pl.BlockSpec((1, 1, N_HEADS * nblk, tm), lambda i: (i // tiles_per_seq, i % tiles_per_seq, 0, 0)),
                      row(N_HEADS * LANES),
                      pl.BlockSpec((1, N_HEADS, tm), lambda i: (i // tiles_per_seq, 0, i % tiles_per_seq))]
        scratch = [pltpu.VMEM((nblk, WIDTH), F32), pltpu.VMEM((1, LANES), F32)]
    else:
        out_shape = [f32_tok, f32_tok, f32_tok, f32_tok, jax.ShapeDtypeStruct((m, N_HEADS), F32),
                     f32_tok, f32_tok, bf_tok, bf_tok, jax.ShapeDtypeStruct((m, 2 * N_BRANCH * WIDTH), BF16)]
        out_specs = [row(WIDTH), row(WIDTH), row(WIDTH), row(WIDTH), row(N_HEADS),
                     row(WIDTH), row(WIDTH), row(WIDTH), row(WIDTH), row(2 * N_BRANCH * WIDTH)]
    return pl.pallas_call(
        functools.partial(_proj_kernel, prompt, tiles_per_seq),
        grid=(n_tiles,),
        in_specs=in_specs,
        out_specs=out_specs,
        out_shape=out_shape,
        scratch_shapes=scratch,
        compiler_params=_cparams(("arbitrary",), 56),
        name="proj_in_prompt" if prompt else "proj_in_sample",
    )(x, shift, scale, gain, w_pad, bf_pad, gqa, gka, gqb, gkb, cos_t, sin_t, gsum)


def _prompt_attn_body(fox, online, hp, i, bound_ref, q_ref, k_ref, vt_ref, b_ref, cq_ref, o_ref, scratch):
    tq = q_ref.shape[1]
    bk = tq
    q = q_ref[0]
    lane = _iota(q.shape, 1)
    zero = jnp.zeros_like(q)
    qm = (jnp.where(lane < HEAD_DIM, q, zero), jnp.where(lane >= HEAD_DIM, q, zero))
    nblk = b_ref.shape[2] // 2 if not fox else 0
    bound = 0.0 if online else bound_ref[0, 0]
    if fox and not online:
        rq = [cq_ref[0, pl.ds(2 * hp + a, 1), :] - bound for a in range(2)]

    def exponents(a, j, diag):
        j0 = pl.multiple_of(j * bk, bk)
        s = _dot_nt(k_ref[0, pl.ds(j0, bk), :], qm[a])
        if fox:
            c = b_ref[0, pl.ds(j0, bk), a * LANES:(a + 1) * LANES]
            if not online:
                s = s + rq[a]
            s = s - jnp.concatenate([c] * (tq // LANES), axis=1)
        elif not diag:
            s = s + (b_ref[0, 0, pl.ds(a * nblk + j, 1), :] - bound)
        elif not online:
            s = s - bound
        if diag:
            s = jnp.where(_iota(s.shape, 0) <= _iota(s.shape, 1), s, NEG_INF)
        return s

    def values(a, j):
        j0 = pl.multiple_of(j * bk, bk)
        return vt_ref[0, a * HEAD_DIM:(a + 1) * HEAD_DIM, pl.ds(j0, bk)]

    if online:
        def update(state, j, diag):
            out = []
            for a in range(2):
                m, l, acc = state[a]
                e = exponents(a, j, diag)
                m_new = jnp.maximum(m, jnp.max(e, axis=0, keepdims=True))
                alpha = jnp.exp2(m - m_new)
                p = jnp.exp2(e - m_new)
                l = alpha * l + jnp.sum(p, axis=0, keepdims=True)
                acc = alpha * acc + _dot(values(a, j), p.astype(BF16))
                out.append((m_new, l, acc))
            return tuple(out)

        init = tuple((jnp.full((1, tq), NEG_INF, F32), jnp.zeros((1, tq), F32), jnp.zeros((HEAD_DIM, tq), F32))
                     for _ in range(2))
        state = update(init, i, True)
        state = lax.fori_loop(0, i, lambda j, st: update(st, j, False), state)
        outs = [acc / l for _, l, acc in state]
    else:
        l_ref, acc_ref = scratch

        def add_blocks(js, diag_last, first):
            es = [[exponents(a, j, diag_last and idx == len(js) - 1) for a in range(2)]
                  for idx, j in enumerate(js)]
            ps = [[jnp.exp2(e) for e in pair] for pair in es]
            for a in range(2):
                l_new = sum(jnp.sum(pair[a].reshape(bk // 8, 8, tq), axis=0) for pair in ps)
                acc_new = sum(_dot(values(a, j), pair[a].astype(BF16)) for j, pair in zip(js, ps))
                if first:
                    l_ref[a] = l_new
                    acc_ref[a] = acc_new
                else:
                    l_ref[a] += l_new
                    acc_ref[a] += acc_new

        u_blocks = ATTN_BLOCKS_PER_STEP
        rem = i % u_blocks
        for r in range(u_blocks):
            @pl.when(rem == r)
            def _():
                add_blocks([i - r + u for u in range(r)] + [i], True, True)

        def group(g, carry):
            add_blocks([g * u_blocks + u for u in range(u_blocks)], False, False)
            return carry

        lax.fori_loop(0, i // u_blocks, group, 0)
        outs = [acc_ref[a] / jnp.sum(l_ref[a], axis=0, keepdims=True) for a in range(2)]
    o_ref[0] = jnp.concatenate(outs, axis=0).T.astype(o_ref.dtype)


def _out_kernel(oa_ref, ob_ref, za_ref, zb_ref, bg_ref, x_ref, gate_ref, wb_ref, wo_ref, y_ref):
    d = x_ref.shape[1]
    ga = (oa_ref[...].astype(F32) * za_ref[...].astype(F32)).astype(BF16)
    gb = (ob_ref[...].astype(F32) * zb_ref[...].astype(F32)).astype(BF16)
    ua = _dot(ga, wb_ref[0])
    ub = _dot(gb, wb_ref[1])
    merged = bg_ref[:, :d].astype(F32) * ua + bg_ref[:, d:].astype(F32) * ub
    y_ref[...] = x_ref[...] + gate_ref[0] * _dot(merged.astype(BF16), wo_ref[...])


def _mixer_out(oa, ob, za, zb, bg, x, gate, wb, wo, tiles_per_mod):
    m, d = x.shape
    tm = ROW_TILE
    r = gate.shape[1]
    row = lambda width: pl.BlockSpec((tm, width), lambda i: (i, 0))
    return pl.pallas_call(
        _out_kernel,
        grid=(m // tm,),
        in_specs=[row(WIDTH), row(WIDTH), row(WIDTH), row(WIDTH), row(N_BRANCH * d), row(d),
                  pl.BlockSpec((1, r, d), lambda i: (i // tiles_per_mod, 0, 0)),
                  pl.BlockSpec((N_BRANCH, WIDTH, d), lambda i: (0, 0, 0)),
                  pl.BlockSpec((d, d), lambda i: (0, 0))],
        out_specs=row(d),
        out_shape=jax.ShapeDtypeStruct((m, d), F32),
        compiler_params=_cparams(("parallel",), 48),
        name="mixer_out",
    )(oa, ob, za, zb, bg, x, gate, wb, wo)


def _block_diag_rows(q):
    t_new = q.shape[0]
    keep = (_iota((N_HEADS, WIDTH), 1) >> 6) == _iota((N_HEADS, WIDTH), 0)
    return jnp.concatenate([jnp.where(keep, q[t:t + 1, :], 0.0) for t in range(t_new)], axis=0)


def _collapse_rows(o64):
    rows = o64.shape[0]
    keep = (_iota((rows, WIDTH), 1) >> 6) == (_iota((rows, WIDTH), 0) & (N_HEADS - 1))
    om = jnp.where(keep, o64, 0.0)
    return jnp.concatenate([jnp.sum(om[t * N_HEADS:(t + 1) * N_HEADS], axis=0, keepdims=True)
                            for t in range(rows // N_HEADS)], axis=0)


def _pad_rows(a, rows):
    return jnp.concatenate([a, jnp.zeros((rows - a.shape[0], a.shape[1]), a.dtype)], axis=0)


def _new_token_init(qbd, kn_ref, vn_ref, bias, m_ref, l_ref, acc_ref):
    kn = _pad_rows(kn_ref[0], LANES).astype(BF16)
    vn = _pad_rows(vn_ref[0], LANES).astype(BF16)
    s = _dot_nt(qbd, kn)
    if bias is not None:
        s = s + bias
    s = jnp.where(_iota(s.shape, 1) <= (_iota(s.shape, 0) >> 3), s, NEG_INF)
    m = jnp.max(s, axis=1, keepdims=True)
    p = jnp.exp(s - m)
    m_ref[...] = m
    l_ref[...] = jnp.sum(p, axis=1, keepdims=True)
    acc_ref[...] = _dot(p.astype(BF16), vn)


def _online_update(s, vt, m_ref, l_ref, acc_ref):
    m_old = m_ref[...]
    m_new = jnp.maximum(m_old, jnp.max(s, axis=1, keepdims=True))
    alpha = jnp.exp(m_old - m_new)
    p = jnp.exp(s - m_new)
    m_ref[...] = m_new
    l_ref[...] = alpha * l_ref[...] + jnp.sum(p, axis=1, keepdims=True)
    acc_ref[...] = alpha * acc_ref[...] + _dot_nt(p.astype(BF16), vt)


def _pages(page_refs):
    return jnp.concatenate([r[0] for r in page_refs], axis=1)


def _moba_sample_body(group, steps, n_blocks, s, q_ref, kn_ref, vn_ref, k_refs, v_refs, o_ref, scratch):
    bm_ref, sc_ref, sel_ref, m_ref, l_ref, acc_ref = scratch
    ppb = MOBA_BLOCK // PAGE_SIZE
    keys = group * PAGE_SIZE
    q = q_ref[0]

    @pl.when(s == 0)
    def _():
        bm_ref[...] = jnp.zeros_like(bm_ref)

    @pl.when(s < steps)
    def _():
        lane_blk = _iota(bm_ref.shape, 1)
        for jb in range(group // ppb):
            tot = k_refs[ppb * jb][0]
            for e in range(1, ppb):
                tot = tot + k_refs[ppb * jb + e][0]
            mean = jnp.sum(tot, axis=1, keepdims=True) * (1.0 / MOBA_BLOCK)
            bm_ref[...] = jnp.where(lane_blk == s * (group // ppb) + jb, mean, bm_ref[...])
        qbd = _block_diag_rows(q * HEAD_DIM ** -0.5).astype(BF16)
        k0 = pl.multiple_of(s * keys, keys)
        sc_ref[:, pl.ds(k0, keys)] = _dot(qbd, _pages(k_refs).astype(BF16))

    @pl.when(s == steps - 1)
    def _():
        g = _dot(_block_diag_rows(q), bm_ref[...], HIGHEST)
        lane = _iota(g.shape, 1)
        lf = lane.astype(F32)
        valid = lane < n_blocks
        g = jnp.where(valid, g, NEG_INF)
        sel = jnp.zeros(g.shape, F32)
        for _ in range(min(MOBA_TOPK, n_blocks)):
            m = jnp.max(g, axis=1, keepdims=True)
            idx = jnp.min(jnp.where(g == m, lf, float(LANES)), axis=1, keepdims=True)
            pick = lf == idx
            sel = jnp.where(pick & valid, 1.0, sel)
            g = jnp.where(pick, NEG_INF, g)
        sel_ref[...] = sel
        qbd = _block_diag_rows(q * HEAD_DIM ** -0.5).astype(BF16)
        _new_token_init(qbd, kn_ref, vn_ref, None, m_ref, l_ref, acc_ref)

    @pl.when(s >= steps)
    def _():
        sv = s - steps
        k0 = pl.multiple_of(sv * keys, keys)
        blk_of_key = sv * (keys // MOBA_BLOCK) + (_iota((LANES, keys), 1) >> 8)
        expand = (_iota((LANES, keys), 0) == blk_of_key).astype(BF16)
        chosen = _dot(sel_ref[...].astype(BF16), expand)
        sc = jnp.where(chosen > 0.5, sc_ref[:, pl.ds(k0, keys)], NEG_INF)
        _online_update(sc, _pages(v_refs).astype(BF16), m_ref, l_ref, acc_ref)

    @pl.when(s == 2 * steps - 1)
    def _():
        o_ref[0] = _collapse_rows(acc_ref[...] / l_ref[...])


def _sample_scratch(t_new):
    rows = t_new * N_HEADS
    return [pltpu.VMEM((rows, 1), F32), pltpu.VMEM((rows, 1), F32), pltpu.VMEM((rows, WIDTH), F32)]


def _suffix_kernel(n_pages, pt_ref, *refs):
    su_ref = refs[n_pages]
    o_ref = refs[n_pages + 1]
    x = jnp.concatenate([refs[g][0] for g in range(n_pages)], axis=0)
    later_in_page = (_iota((PAGE_SIZE, PAGE_SIZE), 0) > _iota((PAGE_SIZE, PAGE_SIZE), 1)).astype(F32)
    in_page = _dot(x, later_in_page, HIGHEST)
    totals = jnp.broadcast_to(jnp.sum(x, axis=1, keepdims=True), x.shape)
    later_pages = _dot(su_ref[...], totals, HIGHEST)
    o_ref[0] = (in_page + later_pages).reshape(n_pages, N_HEADS, PAGE_SIZE)


def _fox_suffix(cache_logf_t, pt_flat, b, pages_per_seq):
    rows = pages_per_seq * N_HEADS
    r = jnp.arange(rows)
    su = ((r[None, :] > r[:, None]) & (r[None, :] % N_HEADS == r[:, None] % N_HEADS)).astype(F32)
    grid_spec = pltpu.PrefetchScalarGridSpec(
        num_scalar_prefetch=1,
        grid=(b,),
        in_specs=[pl.BlockSpec((1, N_HEADS, PAGE_SIZE),
                               functools.partial(lambda g, bb, pt: (pt[bb * pages_per_seq + g], 0, 0), g))
                  for g in range(pages_per_seq)]
        + [pl.BlockSpec((rows, rows), lambda bb, pt: (0, 0))],
        out_specs=pl.BlockSpec((1, pages_per_seq, N_HEADS, PAGE_SIZE), lambda bb, pt: (bb, 0, 0, 0)),
    )
    return pl.pallas_call(
        functools.partial(_suffix_kernel, pages_per_seq),
        grid_spec=grid_spec,
        out_shape=jax.ShapeDtypeStruct((b, pages_per_seq, N_HEADS, PAGE_SIZE), F32),
        compiler_params=_cparams(("parallel",), 32),
        name="fox_sample_suffix",
    )(pt_flat, *([cache_logf_t] * pages_per_seq), su)


def _fox_sample_body(group, steps, s, q_ref, kn_ref, vn_ref, lfn_ref, sfx_ref, k_refs, v_refs, o_ref, scratch):
    m_ref, l_ref, acc_ref = scratch
    t_new = q_ref.shape[1]
    qbd = _block_diag_rows(q_ref[0] * HEAD_DIM ** -0.5).astype(BF16)

    @pl.when(s == 0)
    def _():
        upto = (_iota((LANES, LANES), 0) <= _iota((LANES, LANES), 1)).astype(F32)
        pre_t = _dot(lfn_ref[0], upto, HIGHEST)
        _new_token_init(qbd, kn_ref, vn_ref, -jnp.concatenate([pre_t] * t_new, axis=0), m_ref, l_ref, acc_ref)

    kt = _pages(k_refs).astype(BF16)
    vt = _pages(v_refs).astype(BF16)
    bias = jnp.concatenate([jnp.concatenate([sfx_ref[0, g]] * t_new, axis=0) for g in range(group)], axis=1)
    sc = _dot(qbd, kt) + bias
    _online_update(sc, vt, m_ref, l_ref, acc_ref)

    @pl.when(s == steps - 1)
    def _():
        o_ref[0] = _collapse_rows(acc_ref[...] / l_ref[...])


def _mixer_attn_kernel(fox, online, cfg, pt_ref, bound_ref, *refs):
    group, tiles = cfg["group"], cfg["tiles"]
    n_p_in = 5 if fox else 4
    q_ref, k_ref, vt_ref, b_ref = refs[:4]
    cq_ref = refs[4] if fox else None
    n_s_in = (5 if fox else 3) + 2 * group
    s_in = refs[n_p_in:n_p_in + n_s_in]
    op_ref, os_ref = refs[n_p_in + n_s_in:n_p_in + n_s_in + 2]
    scratch = refs[n_p_in + n_s_in + 2:]
    n_p_scratch = 0 if online else 2
    t = pl.program_id(0)

    @pl.when(t < cfg["prompt_steps"])
    def _():
        _prompt_attn_body(fox, online, (t // tiles) % (N_HEADS // 2), t % tiles, bound_ref,
                          q_ref, k_ref, vt_ref, b_ref, cq_ref, op_ref, scratch[:n_p_scratch])

    @pl.when(t < cfg["sample_steps"])
    def _():
        s = t % cfg["steps_per_sample"]
        pages = s_in[-2 * group:]
        if fox:
            _fox_sample_body(group, cfg["steps_per_sample"], s, *s_in[:5], pages[:group], pages[group:], os_ref,
                             scratch[n_p_scratch:])
        else:
            _moba_sample_body(group, cfg["steps_per_sample"] // 2, cfg["n_blocks"], s, *s_in[:3],
                              pages[:group], pages[group:], os_ref, scratch[n_p_scratch:])


def _mixer_attn(fox, online, bound, pt_flat, prompt_args, sample_args, cache_k, cache_v, pages_per_seq, group):
    q = prompt_args[0]
    n, t_len, _ = q.shape
    tq = MOBA_BLOCK
    tiles = t_len // tq
    b, t_new, _ = sample_args[0].shape
    rows = t_new * N_HEADS
    page_steps = pages_per_seq // group
    steps_per_sample = page_steps if fox else 2 * page_steps
    prompt_steps = n * (N_HEADS // 2) * tiles
    sample_steps = b * steps_per_sample
    total = max(prompt_steps, sample_steps)
    cfg = dict(group=group, tiles=tiles, prompt_steps=prompt_steps, sample_steps=sample_steps,
               steps_per_sample=steps_per_sample, n_blocks=pages_per_seq * PAGE_SIZE // MOBA_BLOCK)

    def p_idx(t):
        tp = jnp.minimum(t, prompt_steps - 1)
        return tp // (tiles * (N_HEADS // 2)), (tp // tiles) % (N_HEADS // 2), tp % tiles

    def s_idx(t):
        ts = jnp.minimum(t, sample_steps - 1)
        return ts // steps_per_sample, ts % steps_per_sample

    def pspec(shape, f):
        return pl.BlockSpec(shape, lambda t, pt: f(*p_idx(t)))

    def sspec(shape, f):
        return pl.BlockSpec(shape, lambda t, pt: f(*s_idx(t)))

    def page(g, phase):
        def index(t, pt):
            bb, s = s_idx(t)
            if fox:
                step = s
            else:
                step = jnp.minimum(s, page_steps - 1) if phase == 0 else jnp.maximum(s - page_steps, 0)
            return (pt[bb * pages_per_seq + step * group + g], 0, 0)
        return pl.BlockSpec((1, WIDTH, PAGE_SIZE), index)

    in_specs = [pl.BlockSpec(memory_space=pltpu.SMEM),
                pspec((1, tq, LANES), lambda bn, hp, i: (bn, i, hp)),
                pspec((1, t_len, LANES), lambda bn, hp, i: (bn, 0, hp)),
                pspec((1, LANES, t_len), lambda bn, hp, i: (bn, hp, 0))]
    if fox:
        in_specs += [pspec((1, t_len, 2 * LANES), lambda bn, hp, i: (bn, 0, hp)),
                     pspec((1, N_HEADS, tq), lambda bn, hp, i: (bn, 0, i))]
    else:
        nblk = t_len // MOBA_BLOCK
        in_specs.append(pspec((1, 1, 2 * nblk, tq), lambda bn, hp, i: (bn, i, hp, 0)))
    tok = sspec((1, t_new, WIDTH), lambda bb, s: (bb, 0, 0))
    in_specs += [tok, tok, tok]
    if fox:
        in_specs += [sspec((1, N_HEADS, LANES), lambda bb, s: (bb, 0, 0)),
                     sspec((1, group, N_HEADS, PAGE_SIZE), lambda bb, s: (bb, s, 0, 0))]
    in_specs += [page(g, 0) for g in range(group)] + [page(g, 1) for g in range(group)]
    scratch = [] if online else [pltpu.VMEM((2, 8, tq), F32), pltpu.VMEM((2, HEAD_DIM, tq), F32)]
    if not fox:
        scratch += [pltpu.VMEM((WIDTH, LANES), F32), pltpu.VMEM((rows, pages_per_seq * PAGE_SIZE), F32),
                    pltpu.VMEM((rows, LANES), F32)]
    scratch += _sample_scratch(t_new)
    grid_spec = pltpu.PrefetchScalarGridSpec(
        num_scalar_prefetch=1,
        grid=(total,),
        in_specs=in_specs,
        out_specs=[pspec((1, tq, LANES), lambda bn, hp, i: (bn, i, hp)), tok],
        scratch_shapes=scratch,
    )
    name = ("fox" if fox else "moba") + "_attn" + ("_online" if online else "")
    return pl.pallas_call(
        functools.partial(_mixer_attn_kernel, fox, online, cfg),
        grid_spec=grid_spec,
        out_shape=[jax.ShapeDtypeStruct((n, t_len, WIDTH), BF16), jax.ShapeDtypeStruct((b, t_new, WIDTH), F32)],
        compiler_params=_cparams(("arbitrary",), 56),
        name=name,
    )(pt_flat, bound, *prompt_args, *sample_args, *([cache_k] * group), *([cache_v] * group))


def _mixer_attn_any_gain(fox, gq, gk, *args):
    slack = 1.02
    limit = 40.0
    bound = (HEAD_DIM ** 0.5) * jnp.max(jnp.abs(gq)) * jnp.max(jnp.abs(gk)) * slack
    bound2 = (bound * LOG2E).astype(F32).reshape(1, 1)
    return lax.cond(bound <= limit,
                    lambda: _mixer_attn(fox, False, bound2, *args),
                    lambda: _mixer_attn(fox, True, bound2, *args))


def _rope_tables(pos):
    half = HEAD_DIM // 2
    inv_freq = ROPE_THETA ** (-jnp.arange(half, dtype=F32) / half)
    ang = pos.astype(F32)[:, None] * inv_freq[None, :]
    cos = jnp.cos(ang)
    sin = jnp.sin(ang)
    cos_t = jnp.tile(cos, (1, LANES // half))
    sin_t = jnp.tile(jnp.concatenate([-sin, sin], axis=1), (1, LANES // HEAD_DIM))
    return cos_t, sin_t


def kernel(x_prompt, x_sample, cache_moba_k, cache_moba_v, cache_fox_k, cache_fox_v, cache_fox_logf, page_table, c_prompt, c_sample, norm_gain, w_ada, b_ada, w_in, b_forget, q_norm_a, k_norm_a, q_norm_b, k_norm_b, w_branch, w_out):
    n, t, d = x_prompt.shape
    b, t_new, _ = x_sample.shape
    depth = w_in.shape[0]
    n_pool = cache_moba_k.shape[1]
    pages = page_table.shape[1]
    past_len = pages * PAGE_SIZE
    assert t % MOBA_BLOCK == 0 and past_len % MOBA_BLOCK == 0 and (b * t_new) % ROW_TILE == 0
    assert ROW_TILE % t_new == 0 and t_new == N_HEADS
    group = min(16, pages)
    d_in = w_in.shape[2]
    w_cols = 12 * WIDTH + LANES

    pos_p = jnp.arange(t)
    pos_s = past_len + jnp.arange(t_new)
    cos_p, sin_p = _rope_tables(pos_p)
    cos_s, sin_s = (jnp.tile(a, (ROW_TILE // t_new, 1)) for a in _rope_tables(pos_s))
    lane_head = jnp.arange(WIDTH) // HEAD_DIM
    gsum = (lane_head[:, None] == lane_head[None, :]).astype(BF16)
    pt_flat = page_table.reshape(-1).astype(jnp.int32)

    yp = x_prompt.reshape(n * t, d)
    ys = x_sample.reshape(b * t_new, d)
    outs_p = [[] for _ in range(5)]
    outs_s = [[] for _ in range(5)]
    for l in range(depth):
        w_pad = jnp.pad(w_in[l], ((0, 0), (0, w_cols - d_in))).astype(BF16)
        bf_pad = jnp.pad(b_forget[l], (0, LANES - N_HEADS)).reshape(1, LANES)
        gains = [jnp.tile(g[l], N_HEADS).reshape(1, WIDTH) for g in (q_norm_a, k_norm_a, q_norm_b, k_norm_b)]
        gain = norm_gain[l].reshape(1, d)
        wb = w_branch[l].astype(BF16)
        wo = w_out[l].astype(BF16)

        mod = _ada_mod(c_prompt, w_ada[l], b_ada[l])
        shift, scale, gate_p = (mod[:, i * d:(i + 1) * d].reshape(n, 1, d) for i in range(3))
        (ka_t, va_t, kb_t, vb_t, logf_t, qa_s, qb_s, ka_b, kb_b, va_tb, vb_tb, za_p, zb_p, bg_p, selb, crep,
         cum_t) = _proj_in(True, yp, shift, scale, gain, w_pad, bf_pad, *gains, cos_p, sin_p, gsum, n, t)
        mod = _ada_mod(c_sample, w_ada[l], b_ada[l])
        rows = lambda a: jnp.repeat(a, t_new, axis=0).reshape(b * t_new // ROW_TILE, ROW_TILE, d)
        shift, scale, gate_s = (rows(mod[:, i * d:(i + 1) * d]) for i in range(3))
        (ka, va, kb, vb, logf, qa, qb, za_s, zb_s, bg_s) = _proj_in(
            False, ys, shift, scale, gain, w_pad, bf_pad, *gains, cos_s, sin_s, gsum, b, t_new)

        as3 = lambda a: a.reshape(n, t, a.shape[-1])
        tok3 = lambda a: a.reshape(b, t_new, WIDTH)
        page_t = lambda c: c[l].transpose(0, 2, 3, 1).reshape(n_pool, WIDTH, PAGE_SIZE)
        ck_a, cv_a, ck_b, cv_b = (page_t(c) for c in (cache_moba_k, cache_moba_v, cache_fox_k, cache_fox_v))
        sfx = _fox_suffix(cache_fox_logf[l].transpose(0, 2, 1), pt_flat, b, pages)
        lfn_t = jnp.pad(logf.reshape(b, t_new, N_HEADS).transpose(0, 2, 1), ((0, 0), (0, 0), (0, LANES - t_new)))
        oa_p, oa_s = _mixer_attn_any_gain(
            False, q_norm_a[l], k_norm_a[l], pt_flat, (as3(qa_s), as3(ka_b), va_tb, selb),
            (tok3(qa), tok3(ka), tok3(va)), ck_a, cv_a, pages, group)
        ob_p, ob_s = _mixer_attn_any_gain(
            True, q_norm_b[l], k_norm_b[l], pt_flat, (as3(qb_s), as3(kb_b), vb_tb, as3(crep), cum_t),
            (tok3(qb), tok3(kb), tok3(vb), lfn_t, sfx), ck_b, cv_b, pages, max(group // 2, 1))

        yp = _mixer_out(oa_p.reshape(n * t, WIDTH), ob_p.reshape(n * t, WIDTH), za_p, zb_p, bg_p, yp, gate_p, wb, wo,
                        t // ROW_TILE)
        ys = _mixer_out(oa_s.reshape(b * t_new, WIDTH).astype(BF16), ob_s.reshape(b * t_new, WIDTH).astype(BF16),
                        za_s, zb_s, bg_s, ys, gate_s, wb, wo, 1)
        for dst, a in zip(outs_p, (ka_t, va_t, kb_t, vb_t)):
            dst.append(a.reshape(n, N_HEADS, HEAD_DIM, t).transpose(0, 3, 1, 2))
        outs_p[4].append(logf_t.transpose(0, 2, 1))
        for dst, a in zip(outs_s, (ka, va, kb, vb)):
            dst.append(a.reshape(b, t_new, N_HEADS, HEAD_DIM))
        outs_s[4].append(logf.reshape(b, t_new, N_HEADS))

    return (yp.reshape(n, t, d), ys.reshape(b, t_new, d),
            *(jnp.stack(o) for o in outs_p), *(jnp.stack(o) for o in outs_s))
```

```python
import functools

import jax
import jax.numpy as jnp
from jax import lax
from jax.experimental import pallas as pl
from jax.experimental.pallas import tpu as pltpu

F32 = jnp.float32
BF16 = jnp.bfloat16
HIGHEST = lax.Precision.HIGHEST
NEG_INF = float("-inf")

HEAD_DIM = 64
N_HEADS = 8
WIDTH = N_HEADS * HEAD_DIM
N_BRANCH = 2
MOBA_BLOCK = 256
MOBA_TOPK = 3
PAGE_SIZE = 128
ROPE_THETA = 10000.0
EPS = 1e-6
LOG2E = 1.4426950408889634
LANES = 128
ROW_TILE = 256
ATTN_BLOCKS_PER_STEP = 8
MiB = 1024 * 1024


def _cparams(dims, vmem_mib):
    return pltpu.CompilerParams(dimension_semantics=dims, vmem_limit_bytes=vmem_mib * MiB)


def _dot(a, b, precision=None):
    return jnp.dot(a, b, preferred_element_type=F32, precision=precision)


def _dot_nt(a, b, precision=None):
    return lax.dot_general(a, b, (((1,), (1,)), ((), ())), preferred_element_type=F32, precision=precision)


def _iota(shape, dim):
    return lax.broadcasted_iota(jnp.int32, shape, dim)


def _split_bf16(x, terms):
    parts = []
    for _ in range(terms):
        p = x.astype(BF16)
        parts.append(p)
        x = x - p.astype(F32)
    return parts


def _ada_kernel(c_ref, w_ref, b_ref, o_ref):
    c = c_ref[...]
    a = c * jax.nn.sigmoid(c)
    o_ref[...] = _dot(a, w_ref[...], HIGHEST) + b_ref[...]


def _ada_mod(c, w_ada, b_ada):
    n, d = c.shape
    d3 = w_ada.shape[1]
    tn = 512
    return pl.pallas_call(
        _ada_kernel,
        grid=(d3 // tn,),
        in_specs=[pl.BlockSpec((n, d), lambda j: (0, 0)),
                  pl.BlockSpec((d, tn), lambda j: (0, j)),
                  pl.BlockSpec((1, tn), lambda j: (0, j))],
        out_specs=pl.BlockSpec((n, tn), lambda j: (0, j)),
        out_shape=jax.ShapeDtypeStruct((n, d3), F32),
        compiler_params=_cparams(("parallel",), 32),
        name="ada_mod",
    )(c, w_ada, b_ada.reshape(1, d3))


def _log_sigmoid(v):
    return jnp.minimum(v, 0.0) - jnp.log1p(jnp.exp(-jnp.abs(v)))


def _proj_kernel(prompt, tiles_per_seq, *refs):
    (x_ref, shift_ref, scale_ref, gain_ref, w_ref, bf_ref, gqa_ref, gka_ref, gqb_ref, gkb_ref,
     cos_ref, sin_ref, gsum_ref) = refs[:13]
    if prompt:
        (kat_ref, vat_ref, kbt_ref, vbt_ref, logft_ref, qa_ref, qb_ref, kab_ref, kbb_ref, vatb_ref, vbtb_ref,
         za_ref, zb_ref, bg_ref, selb_ref, crep_ref, cumt_ref, means_ref, carry_ref) = refs[13:]
    else:
        (ka_ref, va_ref, kb_ref, vb_ref, logf_ref, qa_ref, qb_ref, za_ref, zb_ref, bg_ref) = refs[13:]
    tm = x_ref.shape[0]
    it = pl.program_id(0) % tiles_per_seq

    x = x_ref[...]
    ms = jnp.mean(x * x, axis=-1, keepdims=True)
    h = x * lax.rsqrt(ms + EPS) * gain_ref[...]
    h = h * (1.0 + scale_ref[0]) + shift_ref[0]
    hb = h.astype(BF16)

    def seg(c0, width=WIDTH):
        return _dot(hb, w_ref[:, c0:c0 + width])

    def head_norm(z, g_ref):
        ss = _dot((z * z).astype(BF16), gsum_ref[...])
        return z * lax.rsqrt(ss * (1.0 / HEAD_DIM) + EPS) * g_ref[...]

    cosf = jnp.concatenate([cos_ref[...]] * (WIDTH // LANES), axis=1)
    sinf = jnp.concatenate([sin_ref[...]] * (WIDTH // LANES), axis=1)
    first_half = (_iota((tm, WIDTH), 1) & (HEAD_DIM - 1)) < (HEAD_DIM // 2)

    def rope(y):
        partner = jnp.where(first_half, pltpu.roll(y, WIDTH - HEAD_DIM // 2, 1), pltpu.roll(y, HEAD_DIM // 2, 1))
        return y * cosf + partner * sinf

    qa = rope(head_norm(seg(0), gqa_ref))
    ka = rope(head_norm(seg(WIDTH), gka_ref))
    va = seg(2 * WIDTH)
    za = seg(3 * WIDTH)
    za_ref[...] = (za * jax.nn.sigmoid(za)).astype(BF16)
    qb = head_norm(seg(4 * WIDTH), gqb_ref)
    kb = head_norm(seg(5 * WIDTH), gkb_ref)
    vb = seg(6 * WIDTH)
    zb = seg(7 * WIDTH)
    zb_ref[...] = (zb * jax.nn.sigmoid(zb)).astype(BF16)
    for s in range(2 * N_BRANCH):
        g = seg(8 * WIDTH + s * WIDTH)
        bg_ref[:, s * WIDTH:(s + 1) * WIDTH] = jax.nn.sigmoid(g).astype(BF16)
    lf = _log_sigmoid(seg(12 * WIDTH, LANES) + bf_ref[...])

    if not prompt:
        ka_ref[...] = ka
        va_ref[...] = va
        kb_ref[...] = kb
        vb_ref[...] = vb
        logf_ref[...] = lf[:, :N_HEADS]
        qa_ref[...] = qa
        qb_ref[...] = qb
        return

    scale = HEAD_DIM ** -0.5 * LOG2E
    qa_ref[...] = (qa * scale).astype(BF16)
    qb_ref[...] = (qb * scale).astype(BF16)
    kab_ref[...] = ka.astype(BF16)
    kbb_ref[...] = kb.astype(BF16)
    kat_ref[0] = ka.T
    kbt_ref[0] = kb.T
    va_t = va.T
    vb_t = vb.T
    vat_ref[0] = va_t
    vbt_ref[0] = vb_t
    vatb_ref[0] = va_t.astype(BF16)
    vbtb_ref[0] = vb_t.astype(BF16)
    logft_ref[0] = lf.T[:N_HEADS, :]

    nblk = means_ref.shape[0]

    @pl.when(it == 0)
    def _():
        means_ref[...] = jnp.zeros_like(means_ref)
        carry_ref[...] = jnp.zeros_like(carry_ref)

    tri = (_iota((tm, tm), 1) <= _iota((tm, tm), 0)).astype(BF16)
    cum = sum(_dot(tri, part) for part in _split_bf16(lf, 3)) + carry_ref[...]
    carry_ref[...] = cum[tm - 1:tm, :]
    cum2 = cum * LOG2E
    cumt_ref[0] = cum2.T[:N_HEADS, :]
    for hh in range(N_HEADS):
        crep_ref[:, hh * LANES:(hh + 1) * LANES] = jnp.broadcast_to(cum2[:, hh:hh + 1], (tm, LANES))

    mt = means_ref[...]
    mbd = jnp.concatenate([mt] * N_HEADS, axis=0)
    rr = _iota(mbd.shape, 0)
    ll = _iota(mbd.shape, 1)
    head_of_row = jnp.zeros(mbd.shape, jnp.int32)
    for hh in range(1, N_HEADS):
        head_of_row = head_of_row + (rr >= hh * nblk).astype(jnp.int32)
    mbd = jnp.where(head_of_row == (ll >> 6), mbd, 0.0)
    m_hi, m_lo = _split_bf16(mbd, 2)
    q_hi, q_lo = _split_bf16(qa, 2)
    st = _dot_nt(m_hi, q_hi) + (_dot_nt(m_hi, q_lo) + _dot_nt(m_lo, q_hi))
    jj = _iota((nblk, tm), 0)
    jf = jj.astype(F32)
    valid = jj < it
    for hh in range(N_HEADS):
        s = jnp.where(valid, st[hh * nblk:(hh + 1) * nblk, :], NEG_INF)
        sel = jnp.zeros((nblk, tm), F32)
        for _ in range(MOBA_TOPK):
            m = jnp.max(s, axis=0, keepdims=True)
            idx = jnp.min(jnp.where(s == m, jf, float(nblk)), axis=0, keepdims=True)
            pick = jf == idx
            sel = jnp.where(pick & valid, 1.0, sel)
            s = jnp.where(pick, NEG_INF, s)
        selb_ref[0, 0, hh * nblk:(hh + 1) * nblk, :] = jnp.where(sel > 0.5, 0.0, NEG_INF)
    means_ref[pl.ds(it, 1), :] = jnp.mean(ka, axis=0, keepdims=True)


def _proj_in(prompt, x, shift, scale, gain, w_pad, bf_pad, gqa, gka, gqb, gkb, cos_t, sin_t, gsum, n_seq, seq_len):
    m, d = x.shape
    tm = ROW_TILE
    n_tiles = m // tm
    if prompt:
        tiles_per_seq = seq_len // tm
        tiles_per_mod = tiles_per_seq
    else:
        tiles_per_seq = 1
        tiles_per_mod = 1
    r = shift.shape[1]
    tab_tiles = cos_t.shape[0] // tm
    wp = w_pad.shape[1]
    row = lambda width: pl.BlockSpec((tm, width), lambda i: (i, 0))
    const = lambda shape: pl.BlockSpec(shape, lambda i: (0,) * len(shape))
    in_specs = [
        row(d),
        pl.BlockSpec((1, r, d), lambda i: (i // tiles_per_mod, 0, 0)),
        pl.BlockSpec((1, r, d), lambda i: (i // tiles_per_mod, 0, 0)),
        const((1, d)),
        const((d, wp)),
        const((1, LANES)),
        const((1, WIDTH)), const((1, WIDTH)), const((1, WIDTH)), const((1, WIDTH)),
        pl.BlockSpec((tm, LANES), lambda i: (i % tab_tiles, 0)),
        pl.BlockSpec((tm, LANES), lambda i: (i % tab_tiles, 0)),
        const((WIDTH, WIDTH)),
    ]
    f32_tok = jax.ShapeDtypeStruct((m, WIDTH), F32)
    bf_tok = jax.ShapeDtypeStruct((m, WIDTH), BF16)
    scratch = []
    if prompt:
        nblk = seq_len // MOBA_BLOCK
        vt = jax.ShapeDtypeStruct((n_seq, WIDTH, seq_len), BF16)
        ft = jax.ShapeDtypeStruct((n_seq, WIDTH, seq_len), F32)
        vt_spec = pl.BlockSpec((1, WIDTH, tm), lambda i: (i // tiles_per_seq, 0, i % tiles_per_seq))
        out_shape = [ft, ft, ft, ft, jax.ShapeDtypeStruct((n_seq, N_HEADS, seq_len), F32)]
        out_specs = [vt_spec, vt_spec, vt_spec, vt_spec,
                     pl.BlockSpec((1, N_HEADS, tm), lambda i: (i // tiles_per_seq, 0, i % tiles_per_seq))]
        out_shape += [bf_tok, bf_tok, bf_tok, bf_tok, vt, vt, bf_tok, bf_tok,
                      jax.ShapeDtypeStruct((m, 2 * N_BRANCH * WIDTH), BF16),
                      jax.ShapeDtypeStruct((n_seq, nblk, N_HEADS * nblk, tm), F32),
                      jax.ShapeDtypeStruct((m, N_HEADS * LANES), F32),
                      jax.ShapeDtypeStruct((n_seq, N_HEADS, seq_len), F32)]
        out_specs += [row(WIDTH), row(WIDTH), row(WIDTH), row(WIDTH), vt_spec, vt_spec, row(WIDTH), row(WIDTH),
                      row(2 * N_BRANCH * WIDTH),
                      pl.BlockSpec((1, 1, N_HEADS * nblk, tm), lambda i: (i // tiles_per_seq, i % tiles_per_seq, 0, 0)),
                      row(N_HEADS * LANES),
                      pl.BlockSpec((1, N_HEADS, tm), lambda i: (i // tiles_per_seq, 0, i % tiles_per_seq))]
        scratch = [pltpu.VMEM((nblk, WIDTH), F32), pltpu.VMEM((1, LANES), F32)]
    else:
        out_shape = [f32_tok, f32_tok, f32_tok, f32_tok, jax.ShapeDtypeStruct((m, N_HEADS), F32),
                     f32_tok, f32_tok, bf_tok, bf_tok, jax.ShapeDtypeStruct((m, 2 * N_BRANCH * WIDTH), BF16)]
        out_specs = [row(WIDTH), row(WIDTH), row(WIDTH), row(WIDTH), row(N_HEADS),
                     row(WIDTH), row(WIDTH), row(WIDTH), row(WIDTH), row(2 * N_BRANCH * WIDTH)]
    return pl.pallas_call(
        functools.partial(_proj_kernel, prompt, tiles_per_seq),
        grid=(n_tiles,),
        in_specs=in_specs,
        out_specs=out_specs,
        out_shape=out_shape,
        scratch_shapes=scratch,
        compiler_params=_cparams(("arbitrary",), 56),
        name="proj_in_prompt" if prompt else "proj_in_sample",
    )(x, shift, scale, gain, w_pad, bf_pad, gqa, gka, gqb, gkb, cos_t, sin_t, gsum)


def _prompt_attn_body(fox, online, hp, i, bound_ref, q_ref, k_ref, vt_ref, b_ref, cq_ref, o_ref, scratch):
    tq = q_ref.shape[1]
    bk = tq
    q = q_ref[0]
    lane = _iota(q.shape, 1)
    zero = jnp.zeros_like(q)
    qm = (jnp.where(lane < HEAD_DIM, q, zero), jnp.where(lane >= HEAD_DIM, q, zero))
    nblk = b_ref.shape[2] // 2 if not fox else 0
    bound = 0.0 if online else bound_ref[0, 0]
    if fox and not online:
        rq = [cq_ref[0, pl.ds(2 * hp + a, 1), :] - bound for a in range(2)]

    def exponents(a, j, diag):
        j0 = pl.multiple_of(j * bk, bk)
        s = _dot_nt(k_ref[0, pl.ds(j0, bk), :], qm[a])
        if fox:
            c = b_ref[0, pl.ds(j0, bk), a * LANES:(a + 1) * LANES]
            if not online:
                s = s + rq[a]
            s = s - jnp.concatenate([c] * (tq // LANES), axis=1)
        elif not diag:
            s = s + (b_ref[0, 0, pl.ds(a * nblk + j, 1), :] - bound)
        elif not online:
            s = s - bound
        if diag:
            s = jnp.where(_iota(s.shape, 0) <= _iota(s.shape, 1), s, NEG_INF)
        return s

    def values(a, j):
        j0 = pl.multiple_of(j * bk, bk)
        return vt_ref[0, a * HEAD_DIM:(a + 1) * HEAD_DIM, pl.ds(j0, bk)]

    if online:
        def update(state, j, diag):
            out = []
            for a in range(2):
                m, l, acc = state[a]
                e = exponents(a, j, diag)
                m_new = jnp.maximum(m, jnp.max(e, axis=0, keepdims=True))
                alpha = jnp.exp2(m - m_new)
                p = jnp.exp2(e - m_new)
                l = alpha * l + jnp.sum(p, axis=0, keepdims=True)
                acc = alpha * acc + _dot(values(a, j), p.astype(BF16))
                out.append((m_new, l, acc))
            return tuple(out)

        init = tuple((jnp.full((1, tq), NEG_INF, F32), jnp.zeros((1, tq), F32), jnp.zeros((HEAD_DIM, tq), F32))
                     for _ in range(2))
        state = update(init, i, True)
        state = lax.fori_loop(0, i, lambda j, st: update(st, j, False), state)
        outs = [acc / l for _, l, acc in state]
    else:
        l_ref, acc_ref = scratch

        def add_blocks(js, diag_last, first):
            es = [[exponents(a, j, diag_last and idx == len(js) - 1) for a in range(2)]
                  for idx, j in enumerate(js)]
            ps = [[jnp.exp2(e) for e in pair] for pair in es]
            for a in range(2):
                l_new = sum(jnp.sum(pair[a].reshape(bk // 8, 8, tq), axis=0) for pair in ps)
                acc_new = sum(_dot(values(a, j), pair[a].astype(BF16)) for j, pair in zip(js, ps))
                if first:
                    l_ref[a] = l_new
                    acc_ref[a] = acc_new
                else:
                    l_ref[a] += l_new
                    acc_ref[a] += acc_new

        u_blocks = ATTN_BLOCKS_PER_STEP
        rem = i % u_blocks
        for r in range(u_blocks):
            @pl.when(rem == r)
            def _():
                add_blocks([i - r + u for u in range(r)] + [i], True, True)

        def group(g, carry):
            add_blocks([g * u_blocks + u for u in range(u_blocks)], False, False)
            return carry

        lax.fori_loop(0, i // u_blocks, group, 0)
        outs = [acc_ref[a] / jnp.sum(l_ref[a], axis=0, keepdims=True) for a in range(2)]
    o_ref[0] = jnp.concatenate(outs, axis=0).T.astype(o_ref.dtype)


def _out_kernel(oa_ref, ob_ref, za_ref, zb_ref, bg_ref, x_ref, gate_ref, wb_ref, wo_ref, y_ref):
    d = x_ref.shape[1]
    ga = (oa_ref[...].astype(F32) * za_ref[...].astype(F32)).astype(BF16)
    gb = (ob_ref[...].astype(F32) * zb_ref[...].astype(F32)).astype(BF16)
    ua = _dot(ga, wb_ref[0])
    ub = _dot(gb, wb_ref[1])
    merged = bg_ref[:, :d].astype(F32) * ua + bg_ref[:, d:].astype(F32) * ub
    y_ref[...] = x_ref[...] + gate_ref[0] * _dot(merged.astype(BF16), wo_ref[...])


def _mixer_out(oa, ob, za, zb, bg, x, gate, wb, wo, tiles_per_mod):
    m, d = x.shape
    tm = ROW_TILE
    r = gate.shape[1]
    row = lambda width: pl.BlockSpec((tm, width), lambda i: (i, 0))
    return pl.pallas_call(
        _out_kernel,
        grid=(m // tm,),
        in_specs=[row(WIDTH), row(WIDTH), row(WIDTH), row(WIDTH), row(N_BRANCH * d), row(d),
                  pl.BlockSpec((1, r, d), lambda i: (i // tiles_per_mod, 0, 0)),
                  pl.BlockSpec((N_BRANCH, WIDTH, d), lambda i: (0, 0, 0)),
                  pl.BlockSpec((d, d), lambda i: (0, 0))],
        out_specs=row(d),
        out_shape=jax.ShapeDtypeStruct((m, d), F32),
        compiler_params=_cparams(("parallel",), 48),
        name="mixer_out",
    )(oa, ob, za, zb, bg, x, gate, wb, wo)


def _block_diag_rows(q):
    t_new = q.shape[0]
    keep = (_iota((N_HEADS, WIDTH), 1) >> 6) == _iota((N_HEADS, WIDTH), 0)
    return jnp.concatenate([jnp.where(keep, q[t:t + 1, :], 0.0) for t in range(t_new)], axis=0)


def _collapse_rows(o64):
    rows = o64.shape[0]
    keep = (_iota((rows, WIDTH), 1) >> 6) == (_iota((rows, WIDTH), 0) & (N_HEADS - 1))
    om = jnp.where(keep, o64, 0.0)
    return jnp.concatenate([jnp.sum(om[t * N_HEADS:(t + 1) * N_HEADS], axis=0, keepdims=True)
                            for t in range(rows // N_HEADS)], axis=0)


def _pad_rows(a, rows):
    return jnp.concatenate([a, jnp.zeros((rows - a.shape[0], a.shape[1]), a.dtype)], axis=0)


def _new_token_init(qbd, kn_ref, vn_ref, bias, m_ref, l_ref, acc_ref):
    kn = _pad_rows(kn_ref[0], LANES).astype(BF16)
    vn = _pad_rows(vn_ref[0], LANES).astype(BF16)
    s = _dot_nt(qbd, kn)
    if bias is not None:
        s = s + bias
    s = jnp.where(_iota(s.shape, 1) <= (_iota(s.shape, 0) >> 3), s, NEG_INF)
    m = jnp.max(s, axis=1, keepdims=True)
    p = jnp.exp(s - m)
    m_ref[...] = m
    l_ref[...] = jnp.sum(p, axis=1, keepdims=True)
    acc_ref[...] = _dot(p.astype(BF16), vn)


def _online_update(s, vt, m_ref, l_ref, acc_ref):
    m_old = m_ref[...]
    m_new = jnp.maximum(m_old, jnp.max(s, axis=1, keepdims=True))
    alpha = jnp.exp(m_old - m_new)
    p = jnp.exp(s - m_new)
    m_ref[...] = m_new
    l_ref[...] = alpha * l_ref[...] + jnp.sum(p, axis=1, keepdims=True)
    acc_ref[...] = alpha * acc_ref[...] + _dot_nt(p.astype(BF16), vt)


def _pages(page_refs):
    return jnp.concatenate([r[0] for r in page_refs], axis=1)


def _moba_sample_body(group, steps, n_blocks, s, q_ref, kn_ref, vn_ref, k_refs, v_refs, o_ref, scratch):
    bm_ref, sc_ref, m_ref, l_ref, acc_ref = scratch
    ppb = MOBA_BLOCK // PAGE_SIZE
    keys = group * PAGE_SIZE
    q = q_ref[0]

    @pl.when(s == 0)
    def _():
        bm_ref[...] = jnp.zeros_like(bm_ref)

    @pl.when(s < steps)
    def _():
        lane_blk = _iota(bm_ref.shape, 1)
        for jb in range(group // ppb):
            tot = k_refs[ppb * jb][0]
            for e in range(1, ppb):
                tot = tot + k_refs[ppb * jb + e][0]
            mean = jnp.sum(tot, axis=1, keepdims=True) * (1.0 / MOBA_BLOCK)
            bm_ref[...] = jnp.where(lane_blk == s * (group // ppb) + jb, mean, bm_ref[...])
        qbd = _block_diag_rows(q * HEAD_DIM ** -0.5).astype(BF16)
        k0 = pl.multiple_of(s * keys, keys)
        sc_ref[:, pl.ds(k0, keys)] = _dot(qbd, _pages(k_refs).astype(BF16))

    @pl.when(s == steps - 1)
    def _():
        g = _dot(_block_diag_rows(q), bm_ref[...], HIGHEST)
        lane = _iota(g.shape, 1)
        lf = lane.astype(F32)
        valid = lane < n_blocks
        g = jnp.where(valid, g, NEG_INF)
        sel = jnp.zeros(g.shape, F32)
        for _ in range(min(MOBA_TOPK, n_blocks)):
            m = jnp.max(g, axis=1, keepdims=True)
            idx = jnp.min(jnp.where(g == m, lf, float(LANES)), axis=1, keepdims=True)
            pick = lf == idx
            sel = jnp.where(pick & valid, 1.0, sel)
            g = jnp.where(pick, NEG_INF, g)
        for j in range(n_blocks):
            keep = jnp.broadcast_to(sel[:, j:j + 1], (sel.shape[0], MOBA_BLOCK)) > 0.5
            cols = slice(j * MOBA_BLOCK, (j + 1) * MOBA_BLOCK)
            sc_ref[:, cols] = jnp.where(keep, sc_ref[:, cols], NEG_INF)
        qbd = _block_diag_rows(q * HEAD_DIM ** -0.5).astype(BF16)
        _new_token_init(qbd, kn_ref, vn_ref, None, m_ref, l_ref, acc_ref)

    @pl.when(s >= steps)
    def _():
        k0 = pl.multiple_of((s - steps) * keys, keys)
        _online_update(sc_ref[:, pl.ds(k0, keys)], _pages(v_refs).astype(BF16), m_ref, l_ref, acc_ref)

    @pl.when(s == 2 * steps - 1)
    def _():
        o_ref[0] = _collapse_rows(acc_ref[...] / l_ref[...])


def _sample_scratch(t_new):
    rows = t_new * N_HEADS
    return [pltpu.VMEM((rows, 1), F32), pltpu.VMEM((rows, 1), F32), pltpu.VMEM((rows, WIDTH), F32)]


def _suffix_kernel(n_pages, pt_ref, *refs):
    su_ref = refs[n_pages]
    o_ref = refs[n_pages + 1]
    x = jnp.concatenate([refs[g][0] for g in range(n_pages)], axis=0)
    later_in_page = (_iota((PAGE_SIZE, PAGE_SIZE), 0) > _iota((PAGE_SIZE, PAGE_SIZE), 1)).astype(BF16)
    in_page = sum(_dot(part, later_in_page) for part in _split_bf16(x, 3))
    totals = jnp.broadcast_to(jnp.sum(x, axis=1, keepdims=True), x.shape)
    later_pages = sum(_dot(su_ref[...], part) for part in _split_bf16(totals, 3))
    o_ref[0] = (in_page + later_pages).reshape(n_pages, N_HEADS, PAGE_SIZE)


def _fox_suffix(cache_logf_t, pt_flat, b, pages_per_seq):
    rows = pages_per_seq * N_HEADS
    r = jnp.arange(rows)
    su = ((r[None, :] > r[:, None]) & (r[None, :] % N_HEADS == r[:, None] % N_HEADS)).astype(BF16)
    grid_spec = pltpu.PrefetchScalarGridSpec(
        num_scalar_prefetch=1,
        grid=(b,),
        in_specs=[pl.BlockSpec((1, N_HEADS, PAGE_SIZE),
                               functools.partial(lambda g, bb, pt: (pt[bb * pages_per_seq + g], 0, 0), g))
                  for g in range(pages_per_seq)]
        + [pl.BlockSpec((rows, rows), lambda bb, pt: (0, 0))],
        out_specs=pl.BlockSpec((1, pages_per_seq, N_HEADS, PAGE_SIZE), lambda bb, pt: (bb, 0, 0, 0)),
    )
    return pl.pallas_call(
        functools.partial(_suffix_kernel, pages_per_seq),
        grid_spec=grid_spec,
        out_shape=jax.ShapeDtypeStruct((b, pages_per_seq, N_HEADS, PAGE_SIZE), F32),
        compiler_params=_cparams(("parallel",), 32),
        name="fox_sample_suffix",
    )(pt_flat, *([cache_logf_t] * pages_per_seq), su)


def _fox_sample_body(group, steps, s, q_ref, kn_ref, vn_ref, lfn_ref, sfx_ref, k_refs, v_refs, o_ref, scratch):
    m_ref, l_ref, acc_ref = scratch
    t_new = q_ref.shape[1]
    qbd = _block_diag_rows(q_ref[0] * HEAD_DIM ** -0.5).astype(BF16)

    @pl.when(s == 0)
    def _():
        upto = (_iota((LANES, LANES), 0) <= _iota((LANES, LANES), 1)).astype(F32)
        pre_t = _dot(lfn_ref[0], upto, HIGHEST)
        _new_token_init(qbd, kn_ref, vn_ref, -jnp.concatenate([pre_t] * t_new, axis=0), m_ref, l_ref, acc_ref)

    kt = _pages(k_refs).astype(BF16)
    vt = _pages(v_refs).astype(BF16)
    bias = jnp.concatenate([jnp.concatenate([sfx_ref[0, g]] * t_new, axis=0) for g in range(group)], axis=1)
    sc = _dot(qbd, kt) + bias
    _online_update(sc, vt, m_ref, l_ref, acc_ref)

    @pl.when(s == steps - 1)
    def _():
        o_ref[0] = _collapse_rows(acc_ref[...] / l_ref[...])


def _mixer_attn_kernel(fox, online, cfg, pt_ref, bound_ref, *refs):
    group, tiles = cfg["group"], cfg["tiles"]
    n_p_in = 5 if fox else 4
    q_ref, k_ref, vt_ref, b_ref = refs[:4]
    cq_ref = refs[4] if fox else None
    n_s_in = (5 if fox else 3) + 2 * group
    s_in = refs[n_p_in:n_p_in + n_s_in]
    op_ref, os_ref = refs[n_p_in + n_s_in:n_p_in + n_s_in + 2]
    scratch = refs[n_p_in + n_s_in + 2:]
    n_p_scratch = 0 if online else 2
    t = pl.program_id(0)

    @pl.when(t < cfg["prompt_steps"])
    def _():
        _prompt_attn_body(fox, online, (t // tiles) % (N_HEADS // 2), t % tiles, bound_ref,
                          q_ref, k_ref, vt_ref, b_ref, cq_ref, op_ref, scratch[:n_p_scratch])

    @pl.when(t < cfg["sample_steps"])
    def _():
        s = t % cfg["steps_per_sample"]
        pages = s_in[-2 * group:]
        if fox:
            _fox_sample_body(group, cfg["steps_per_sample"], s, *s_in[:5], pages[:group], pages[group:], os_ref,
                             scratch[n_p_scratch:])
        else:
            _moba_sample_body(group, cfg["steps_per_sample"] // 2, cfg["n_blocks"], s, *s_in[:3],
                              pages[:group], pages[group:], os_ref, scratch[n_p_scratch:])


def _mixer_attn(fox, online, bound, pt_flat, prompt_args, sample_args, cache_k, cache_v, pages_per_seq, group):
    q = prompt_args[0]
    n, t_len, _ = q.shape
    tq = MOBA_BLOCK
    tiles = t_len // tq
    b, t_new, _ = sample_args[0].shape
    rows = t_new * N_HEADS
    page_steps = pages_per_seq // group
    steps_per_sample = page_steps if fox else 2 * page_steps
    prompt_steps = n * (N_HEADS // 2) * tiles
    sample_steps = b * steps_per_sample
    total = max(prompt_steps, sample_steps)
    cfg = dict(group=group, tiles=tiles, prompt_steps=prompt_steps, sample_steps=sample_steps,
               steps_per_sample=steps_per_sample, n_blocks=pages_per_seq * PAGE_SIZE // MOBA_BLOCK)

    def p_idx(t):
        tp = jnp.minimum(t, prompt_steps - 1)
        return tp // (tiles * (N_HEADS // 2)), (tp // tiles) % (N_HEADS // 2), tp % tiles

    def s_idx(t):
        ts = jnp.minimum(t, sample_steps - 1)
        return ts // steps_per_sample, ts % steps_per_sample

    def pspec(shape, f):
        return pl.BlockSpec(shape, lambda t, pt: f(*p_idx(t)))

    def sspec(shape, f):
        return pl.BlockSpec(shape, lambda t, pt: f(*s_idx(t)))

    def page(g, phase):
        def index(t, pt):
            bb, s = s_idx(t)
            if fox:
                step = s
            else:
                step = jnp.minimum(s, page_steps - 1) if phase == 0 else jnp.maximum(s - page_steps, 0)
            return (pt[bb * pages_per_seq + step * group + g], 0, 0)
        return pl.BlockSpec((1, WIDTH, PAGE_SIZE), index)

    in_specs = [pl.BlockSpec(memory_space=pltpu.SMEM),
                pspec((1, tq, LANES), lambda bn, hp, i: (bn, i, hp)),
                pspec((1, t_len, LANES), lambda bn, hp, i: (bn, 0, hp)),
                pspec((1, LANES, t_len), lambda bn, hp, i: (bn, hp, 0))]
    if fox:
        in_specs += [pspec((1, t_len, 2 * LANES), lambda bn, hp, i: (bn, 0, hp)),
                     pspec((1, N_HEADS, tq), lambda bn, hp, i: (bn, 0, i))]
    else:
        nblk = t_len // MOBA_BLOCK
        in_specs.append(pspec((1, 1, 2 * nblk, tq), lambda bn, hp, i: (bn, i, hp, 0)))
    tok = sspec((1, t_new, WIDTH), lambda bb, s: (bb, 0, 0))
    in_specs += [tok, tok, tok]
    if fox:
        in_specs += [sspec((1, N_HEADS, LANES), lambda bb, s: (bb, 0, 0)),
                     sspec((1, group, N_HEADS, PAGE_SIZE), lambda bb, s: (bb, s, 0, 0))]
    in_specs += [page(g, 0) for g in range(group)] + [page(g, 1) for g in range(group)]
    scratch = [] if online else [pltpu.VMEM((2, 8, tq), F32), pltpu.VMEM((2, HEAD_DIM, tq), F32)]
    if not fox:
        scratch += [pltpu.VMEM((WIDTH, LANES), F32), pltpu.VMEM((rows, pages_per_seq * PAGE_SIZE), F32)]
    scratch += _sample_scratch(t_new)
    grid_spec = pltpu.PrefetchScalarGridSpec(
        num_scalar_prefetch=1,
        grid=(total,),
        in_specs=in_specs,
        out_specs=[pspec((1, tq, LANES), lambda bn, hp, i: (bn, i, hp)), tok],
        scratch_shapes=scratch,
    )
    name = ("fox" if fox else "moba") + "_attn" + ("_online" if online else "")
    return pl.pallas_call(
        functools.partial(_mixer_attn_kernel, fox, online, cfg),
        grid_spec=grid_spec,
        out_shape=[jax.ShapeDtypeStruct((n, t_len, WIDTH), BF16), jax.ShapeDtypeStruct((b, t_new, WIDTH), F32)],
        compiler_params=_cparams(("arbitrary",), 56),
        name=name,
    )(pt_flat, bound, *prompt_args, *sample_args, *([cache_k] * group), *([cache_v] * group))


def _mixer_attn_any_gain(fox, gq, gk, *args):
    slack = 1.02
    limit = 40.0
    bound = (HEAD_DIM ** 0.5) * jnp.max(jnp.abs(gq)) * jnp.max(jnp.abs(gk)) * slack
    bound2 = (bound * LOG2E).astype(F32).reshape(1, 1)
    return lax.cond(bound <= limit,
                    lambda: _mixer_attn(fox, False, bound2, *args),
                    lambda: _mixer_attn(fox, True, bound2, *args))


def _rope_tables(pos):
    half = HEAD_DIM // 2
    inv_freq = ROPE_THETA ** (-jnp.arange(half, dtype=F32) / half)
    ang = pos.astype(F32)[:, None] * inv_freq[None, :]
    cos = jnp.cos(ang)
    sin = jnp.sin(ang)
    cos_t = jnp.tile(cos, (1, LANES // half))
    sin_t = jnp.tile(jnp.concatenate([-sin, sin], axis=1), (1, LANES // HEAD_DIM))
    return cos_t, sin_t


def kernel(x_prompt, x_sample, cache_moba_k, cache_moba_v, cache_fox_k, cache_fox_v, cache_fox_logf, page_table, c_prompt, c_sample, norm_gain, w_ada, b_ada, w_in, b_forget, q_norm_a, k_norm_a, q_norm_b, k_norm_b, w_branch, w_out):
    n, t, d = x_prompt.shape
    b, t_new, _ = x_sample.shape
    depth = w_in.shape[0]
    n_pool = cache_moba_k.shape[1]
    pages = page_table.shape[1]
    past_len = pages * PAGE_SIZE
    assert t % MOBA_BLOCK == 0 and past_len % MOBA_BLOCK == 0 and (b * t_new) % ROW_TILE == 0
    assert ROW_TILE % t_new == 0 and t_new == N_HEADS
    group = min(16, pages)
    d_in = w_in.shape[2]
    w_cols = 12 * WIDTH + LANES

    pos_p = jnp.arange(t)
    pos_s = past_len + jnp.arange(t_new)
    cos_p, sin_p = _rope_tables(pos_p)
    cos_s, sin_s = (jnp.tile(a, (ROW_TILE // t_new, 1)) for a in _rope_tables(pos_s))
    lane_head = jnp.arange(WIDTH) // HEAD_DIM
    gsum = (lane_head[:, None] == lane_head[None, :]).astype(BF16)
    pt_flat = page_table.reshape(-1).astype(jnp.int32)

    yp = x_prompt.reshape(n * t, d)
    ys = x_sample.reshape(b * t_new, d)
    outs_p = [[] for _ in range(5)]
    outs_s = [[] for _ in range(5)]
    for l in range(depth):
        w_pad = jnp.pad(w_in[l], ((0, 0), (0, w_cols - d_in))).astype(BF16)
        bf_pad = jnp.pad(b_forget[l], (0, LANES - N_HEADS)).reshape(1, LANES)
        gains = [jnp.tile(g[l], N_HEADS).reshape(1, WIDTH) for g in (q_norm_a, k_norm_a, q_norm_b, k_norm_b)]
        gain = norm_gain[l].reshape(1, d)
        wb = w_branch[l].astype(BF16)
        wo = w_out[l].astype(BF16)

        mod = _ada_mod(c_prompt, w_ada[l], b_ada[l])
        shift, scale, gate_p = (mod[:, i * d:(i + 1) * d].reshape(n, 1, d) for i in range(3))
        (ka_t, va_t, kb_t, vb_t, logf_t, qa_s, qb_s, ka_b, kb_b, va_tb, vb_tb, za_p, zb_p, bg_p, selb, crep,
         cum_t) = _proj_in(True, yp, shift, scale, gain, w_pad, bf_pad, *gains, cos_p, sin_p, gsum, n, t)
        mod = _ada_mod(c_sample, w_ada[l], b_ada[l])
        rows = lambda a: jnp.repeat(a, t_new, axis=0).reshape(b * t_new // ROW_TILE, ROW_TILE, d)
        shift, scale, gate_s = (rows(mod[:, i * d:(i + 1) * d]) for i in range(3))
        (ka, va, kb, vb, logf, qa, qb, za_s, zb_s, bg_s) = _proj_in(
            False, ys, shift, scale, gain, w_pad, bf_pad, *gains, cos_s, sin_s, gsum, b, t_new)

        as3 = lambda a: a.reshape(n, t, a.shape[-1])
        tok3 = lambda a: a.reshape(b, t_new, WIDTH)
        page_t = lambda c: c[l].transpose(0, 2, 3, 1).reshape(n_pool, WIDTH, PAGE_SIZE)
        ck_a, cv_a, ck_b, cv_b = (page_t(c) for c in (cache_moba_k, cache_moba_v, cache_fox_k, cache_fox_v))
        sfx = _fox_suffix(cache_fox_logf[l].transpose(0, 2, 1), pt_flat, b, pages)
        lfn_t = jnp.pad(logf.reshape(b, t_new, N_HEADS).transpose(0, 2, 1), ((0, 0), (0, 0), (0, LANES - t_new)))
        oa_p, oa_s = _mixer_attn_any_gain(
            False, q_norm_a[l], k_norm_a[l], pt_flat, (as3(qa_s), as3(ka_b), va_tb, selb),
            (tok3(qa), tok3(ka), tok3(va)), ck_a, cv_a, pages, group)
        ob_p, ob_s = _mixer_attn_any_gain(
            True, q_norm_b[l], k_norm_b[l], pt_flat, (as3(qb_s), as3(kb_b), vb_tb, as3(crep), cum_t),
            (tok3(qb), tok3(kb), tok3(vb), lfn_t, sfx), ck_b, cv_b, pages, max(group // 2, 1))

        yp = _mixer_out(oa_p.reshape(n * t, WIDTH), ob_p.reshape(n * t, WIDTH), za_p, zb_p, bg_p, yp, gate_p, wb, wo,
                        t // ROW_TILE)
        ys = _mixer_out(oa_s.reshape(b * t_new, WIDTH).astype(BF16), ob_s.reshape(b * t_new, WIDTH).astype(BF16),
                        za_s, zb_s, bg_s, ys, gate_s, wb, wo, 1)
        for dst, a in zip(outs_p, (ka_t, va_t, kb_t, vb_t)):
            dst.append(a.reshape(n, N_HEADS, HEAD_DIM, t).transpose(0, 3, 1, 2))
        outs_p[4].append(logf_t.transpose(0, 2, 1))
        for dst, a in zip(outs_s, (ka, va, kb, vb)):
            dst.append(a.reshape(b, t_new, N_HEADS, HEAD_DIM))
        outs_s[4].append(logf.reshape(b, t_new, N_HEADS))

    return (yp.reshape(n, t, d), ys.reshape(b, t_new, d),
            *(jnp.stack(o) for o in outs_p), *(jnp.stack(o) for o in outs_s))
```

```python
import functools

import jax
import jax.numpy as jnp
from jax import lax
from jax.experimental import pallas as pl
from jax.experimental.pallas import tpu as pltpu

F32 = jnp.float32
BF16 = jnp.bfloat16
HIGHEST = lax.Precision.HIGHEST
NEG_INF = float("-inf")

HEAD_DIM = 64
N_HEADS = 8
WIDTH = N_HEADS * HEAD_DIM
N_BRANCH = 2
MOBA_BLOCK = 256
MOBA_TOPK = 3
PAGE_SIZE = 128
ROPE_THETA = 10000.0
EPS = 1e-6
LOG2E = 1.4426950408889634
LANES = 128
ROW_TILE = 256
OUT_ROW_TILE = 512
ATTN_BLOCKS_PER_STEP = 16
MiB = 1024 * 1024


def _cparams(dims, vmem_mib):
    return pltpu.CompilerParams(dimension_semantics=dims, vmem_limit_bytes=vmem_mib * MiB)


def _dot(a, b, precision=None):
    return jnp.dot(a, b, preferred_element_type=F32, precision=precision)


def _dot_nt(a, b, precision=None):
    return lax.dot_general(a, b, (((1,), (1,)), ((), ())), preferred_element_type=F32, precision=precision)


def _iota(shape, dim):
    return lax.broadcasted_iota(jnp.int32, shape, dim)


def _div(x, d):
    if d & (d - 1) == 0:
        return lax.shift_right_logical(x, jnp.int32(d.bit_length() - 1))
    return x // d


def _mod(x, d):
    if d & (d - 1) == 0:
        return x & (d - 1)
    return x % d


def _split_bf16(x, terms):
    parts = []
    for _ in range(terms):
        p = x.astype(BF16)
        parts.append(p)
        x = x - p.astype(F32)
    return parts


def _ada_kernel(c_ref, w_ref, b_ref, o_ref):
    c = c_ref[...]
    a = c * jax.nn.sigmoid(c)
    o_ref[...] = _dot(a, w_ref[...], HIGHEST) + b_ref[...]


def _ada_mod(c, w_ada, b_ada):
    n, d = c.shape
    d3 = w_ada.shape[1]
    tn = 512
    return pl.pallas_call(
        _ada_kernel,
        grid=(d3 // tn,),
        in_specs=[pl.BlockSpec((n, d), lambda j: (0, 0)),
                  pl.BlockSpec((d, tn), lambda j: (0, j)),
                  pl.BlockSpec((1, tn), lambda j: (0, j))],
        out_specs=pl.BlockSpec((n, tn), lambda j: (0, j)),
        out_shape=jax.ShapeDtypeStruct((n, d3), F32),
        compiler_params=_cparams(("parallel",), 32),
        name="ada_mod",
    )(c, w_ada, b_ada.reshape(1, d3))


def _log_sigmoid(v):
    return jnp.minimum(v, 0.0) - jnp.log1p(jnp.exp(-jnp.abs(v)))


def _proj_kernel(prompt, tiles_per_seq, *refs):
    (x_ref, shift_ref, scale_ref, gain_ref, w_ref, bf_ref, gqa_ref, gka_ref, gqb_ref, gkb_ref,
     cos_ref, sin_ref, gsum_ref) = refs[:13]
    if prompt:
        (kat_ref, vat_ref, kbt_ref, vbt_ref, logft_ref, qa_ref, qb_ref, kab_ref, kbb_ref, vatb_ref, vbtb_ref,
         za_ref, zb_ref, bg_ref, selb_ref, crep_ref, cumt_ref, means_ref, carry_ref) = refs[13:]
    else:
        (ka_ref, va_ref, kb_ref, vb_ref, logf_ref, qa_ref, qb_ref, za_ref, zb_ref, bg_ref) = refs[13:]
    tm = x_ref.shape[0]
    it = _mod(pl.program_id(0), tiles_per_seq)

    x = x_ref[...]
    ms = jnp.mean(x * x, axis=-1, keepdims=True)
    h = x * lax.rsqrt(ms + EPS) * gain_ref[...]
    h = h * (1.0 + scale_ref[0]) + shift_ref[0]
    hb = h.astype(BF16)

    def seg(c0, width=WIDTH):
        return _dot(hb, w_ref[:, c0:c0 + width])

    def head_norm(z, g_ref):
        ss = _dot((z * z).astype(BF16), gsum_ref[...])
        return z * lax.rsqrt(ss * (1.0 / HEAD_DIM) + EPS) * g_ref[...]

    cosf = jnp.concatenate([cos_ref[...]] * (WIDTH // LANES), axis=1)
    sinf = jnp.concatenate([sin_ref[...]] * (WIDTH // LANES), axis=1)
    first_half = (_iota((tm, WIDTH), 1) & (HEAD_DIM - 1)) < (HEAD_DIM // 2)

    def rope(y):
        partner = jnp.where(first_half, pltpu.roll(y, WIDTH - HEAD_DIM // 2, 1), pltpu.roll(y, HEAD_DIM // 2, 1))
        return y * cosf + partner * sinf

    lf = _log_sigmoid(seg(12 * WIDTH, LANES) + bf_ref[...])
    if prompt:
        nblk = means_ref.shape[0]

        @pl.when(it == 0)
        def _():
            means_ref[...] = jnp.zeros_like(means_ref)
            carry_ref[...] = jnp.zeros_like(carry_ref)

        tri = (_iota((tm, tm), 1) <= _iota((tm, tm), 0)).astype(BF16)
        cum = sum(_dot(tri, part) for part in _split_bf16(lf, 3)) + carry_ref[...]
        carry_ref[...] = cum[tm - 1:tm, :]
        cum2 = cum * LOG2E
        cumt_ref[0] = cum2.T[:N_HEADS, :]
        for hh in range(N_HEADS):
            crep_ref[:, hh * LANES:(hh + 1) * LANES] = jnp.broadcast_to(cum2[:, hh:hh + 1], (tm, LANES))
        logft_ref[0] = lf.T[:N_HEADS, :]

    qa = rope(head_norm(seg(0), gqa_ref))
    ka = rope(head_norm(seg(WIDTH), gka_ref))
    if prompt:
        mt = means_ref[...]
        mbd = jnp.concatenate([mt] * N_HEADS, axis=0)
        rr = _iota(mbd.shape, 0)
        ll = _iota(mbd.shape, 1)
        head_of_row = jnp.zeros(mbd.shape, jnp.int32)
        for hh in range(1, N_HEADS):
            head_of_row = head_of_row + (rr >= hh * nblk).astype(jnp.int32)
        mbd = jnp.where(head_of_row == (ll >> 6), mbd, 0.0)
        m_hi, m_lo = _split_bf16(mbd, 2)
        q_hi, q_lo = _split_bf16(qa, 2)
        st = _dot_nt(m_hi, q_hi) + (_dot_nt(m_hi, q_lo) + _dot_nt(m_lo, q_hi))
        jj = _iota((nblk, tm), 0)
        jf = jj.astype(F32)
        valid = jj < it
        for hh in range(N_HEADS):
            s = jnp.where(valid, st[hh * nblk:(hh + 1) * nblk, :], NEG_INF)
            sel = jnp.zeros((nblk, tm), F32)
            for _ in range(MOBA_TOPK):
                m = jnp.max(s, axis=0, keepdims=True)
                idx = jnp.min(jnp.where(s == m, jf, float(nblk)), axis=0, keepdims=True)
                pick = jf == idx
                sel = jnp.where(pick & valid, 1.0, sel)
                s = jnp.where(pick, NEG_INF, s)
            selb_ref[0, 0, hh * nblk:(hh + 1) * nblk, :] = jnp.where(sel > 0.5, 0.0, NEG_INF)
        means_ref[pl.ds(it, 1), :] = jnp.mean(ka, axis=0, keepdims=True)

    va = seg(2 * WIDTH)
    za = seg(3 * WIDTH)
    za_ref[...] = (za * jax.nn.sigmoid(za)).astype(BF16)
    qb = head_norm(seg(4 * WIDTH), gqb_ref)
    kb = head_norm(seg(5 * WIDTH), gkb_ref)
    vb = seg(6 * WIDTH)
    zb = seg(7 * WIDTH)
    zb_ref[...] = (zb * jax.nn.sigmoid(zb)).astype(BF16)
    for s in range(2 * N_BRANCH):
        g = seg(8 * WIDTH + s * WIDTH)
        bg_ref[:, s * WIDTH:(s + 1) * WIDTH] = jax.nn.sigmoid(g).astype(BF16)

    if not prompt:
        ka_ref[...] = ka
        va_ref[...] = va
        kb_ref[...] = kb
        vb_ref[...] = vb
        logf_ref[...] = lf[:, :N_HEADS]
        qa_ref[...] = qa
        qb_ref[...] = qb
        return

    scale = HEAD_DIM ** -0.5 * LOG2E
    qa_ref[...] = (qa * scale).astype(BF16)
    qb_ref[...] = (qb * scale).astype(BF16)
    kab_ref[...] = ka.astype(BF16)
    kbb_ref[...] = kb.astype(BF16)
    kat_ref[0] = ka.T
    kbt_ref[0] = kb.T
    va_t = va.T
    vb_t = vb.T
    vat_ref[0] = va_t
    vbt_ref[0] = vb_t
    vatb_ref[0] = va_t.astype(BF16)
    vbtb_ref[0] = vb_t.astype(BF16)


def _proj_in(prompt, x, shift, scale, gain, w_pad, bf_pad, gqa, gka, gqb, gkb, cos_t, sin_t, gsum, n_seq, seq_len):
    m, d = x.shape
    tm = ROW_TILE
    n_tiles = m // tm
    if prompt:
        tiles_per_seq = seq_len // tm
        tiles_per_mod = tiles_per_seq
    else:
        tiles_per_seq = 1
        tiles_per_mod = 1
    r = shift.shape[1]
    tab_tiles = cos_t.shape[0] // tm
    wp = w_pad.shape[1]
    row = lambda width: pl.BlockSpec((tm, width), lambda i: (i, 0))
    const = lambda shape: pl.BlockSpec(shape, lambda i: (0,) * len(shape))
    in_specs = [
        row(d),
        pl.BlockSpec((1, r, d), lambda i: (_div(i, tiles_per_mod), 0, 0)),
        pl.BlockSpec((1, r, d), lambda i: (_div(i, tiles_per_mod), 0, 0)),
        const((1, d)),
        const((d, wp)),
        const((1, LANES)),
        const((1, WIDTH)), const((1, WIDTH)), const((1, WIDTH)), const((1, WIDTH)),
        pl.BlockSpec((tm, LANES), lambda i: (_mod(i, tab_tiles), 0)),
        pl.BlockSpec((tm, LANES), lambda i: (_mod(i, tab_tiles), 0)),
        const((WIDTH, WIDTH)),
    ]
    f32_tok = jax.ShapeDtypeStruct((m, WIDTH), F32)
    bf_tok = jax.ShapeDtypeStruct((m, WIDTH), BF16)
    scratch = []
    if prompt:
        nblk = seq_len // MOBA_BLOCK
        vt = jax.ShapeDtypeStruct((n_seq, WIDTH, seq_len), BF16)
        ft = jax.ShapeDtypeStruct((n_seq, WIDTH, seq_len), F32)
        vt_spec = pl.BlockSpec((1, WIDTH, tm), lambda i: (_div(i, tiles_per_seq), 0, _mod(i, tiles_per_seq)))
        out_shape = [ft, ft, ft, ft, jax.ShapeDtypeStruct((n_seq, N_HEADS, seq_len), F32)]
        out_specs = [vt_spec, vt_spec, vt_spec, vt_spec,
                     pl.BlockSpec((1, N_HEADS, tm), lambda i: (_div(i, tiles_per_seq), 0, _mod(i, tiles_per_seq)))]
        out_shape += [bf_tok, bf_tok, bf_tok, bf_tok, vt, vt, bf_tok, bf_tok,
                      jax.ShapeDtypeStruct((m, 2 * N_BRANCH * WIDTH), BF16),
                      jax.ShapeDtypeStruct((n_seq, nblk, N_HEADS * nblk, tm), F32),
                      jax.ShapeDtypeStruct((m, N_HEADS * LANES), F32),
                      jax.ShapeDtypeStruct((n_seq, N_HEADS, seq_len), F32)]
        out_specs += [row(WIDTH), row(WIDTH), row(WIDTH), row(WIDTH), vt_spec, vt_spec, row(WIDTH), row(WIDTH),
                      row(2 * N_BRANCH * WIDTH),
                      pl.BlockSpec((1, 1, N_HEADS * nblk, tm), lambda i: (_div(i, tiles_per_seq), _mod(i, tiles_per_seq), 0, 0)),
                      row(N_HEADS * LANES),
                      pl.BlockSpec((1, N_HEADS, tm), lambda i: (_div(i, tiles_per_seq), 0, _mod(i, tiles_per_seq)))]
        scratch = [pltpu.VMEM((nblk, WIDTH), F32), pltpu.VMEM((1, LANES), F32)]
    else:
        out_shape = [f32_tok, f32_tok, f32_tok, f32_tok, jax.ShapeDtypeStruct((m, N_HEADS), F32),
                     f32_tok, f32_tok, bf_tok, bf_tok, jax.ShapeDtypeStruct((m, 2 * N_BRANCH * WIDTH), BF16)]
        out_specs = [row(WIDTH), row(WIDTH), row(WIDTH), row(WIDTH), row(N_HEADS),
                     row(WIDTH), row(WIDTH), row(WIDTH), row(WIDTH), row(2 * N_BRANCH * WIDTH)]
    return pl.pallas_call(
        functools.partial(_proj_kernel, prompt, tiles_per_seq),
        grid=(n_tiles,),
        in_specs=in_specs,
        out_specs=out_specs,
        out_shape=out_shape,
        scratch_shapes=scratch,
        compiler_params=_cparams(("arbitrary",), 56),
        name="proj_in_prompt" if prompt else "proj_in_sample",
    )(x, shift, scale, gain, w_pad, bf_pad, gqa, gka, gqb, gkb, cos_t, sin_t, gsum)


def _prompt_attn_body(fox, online, hp, i, bound_ref, q_ref, k_ref, vt_ref, b_ref, cq_ref, o_ref, scratch):
    tq = q_ref.shape[1]
    bk = tq
    q = q_ref[0]
    lane = _iota(q.shape, 1)
    zero = jnp.zeros_like(q)
    qm = (jnp.where(lane < HEAD_DIM, q, zero), jnp.where(lane >= HEAD_DIM, q, zero))
    nblk = b_ref.shape[2] // 2 if not fox else 0
    bound = 0.0 if online else bound_ref[0, 0]
    if fox and not online:
        rq = [cq_ref[0, pl.ds(2 * hp + a, 1), :] - bound for a in range(2)]

    def exponents(a, j, diag):
        j0 = pl.multiple_of(j * bk, bk)
        s = _dot_nt(k_ref[0, pl.ds(j0, bk), :], qm[a])
        if fox:
            c = b_ref[0, pl.ds(j0, bk), a * LANES:(a + 1) * LANES]
            if not online:
                s = s + rq[a]
            s = s - jnp.concatenate([c] * (tq // LANES), axis=1)
        elif not diag:
            s = s + (b_ref[0, 0, pl.ds(a * nblk + j, 1), :] - bound)
        elif not online:
            s = s - bound
        if diag:
            s = jnp.where(_iota(s.shape, 0) <= _iota(s.shape, 1), s, NEG_INF)
        return s

    def values(a, j):
        j0 = pl.multiple_of(j * bk, bk)
        return vt_ref[0, a * HEAD_DIM:(a + 1) * HEAD_DIM, pl.ds(j0, bk)]

    if online:
        def update(state, j, diag):
            out = []
            for a in range(2):
                m, l, acc = state[a]
                e = exponents(a, j, diag)
                m_new = jnp.maximum(m, jnp.max(e, axis=0, keepdims=True))
                alpha = jnp.exp2(m - m_new)
                p = jnp.exp2(e - m_new)
                l = alpha * l + jnp.sum(p, axis=0, keepdims=True)
                acc = alpha * acc + _dot(values(a, j), p.astype(BF16))
                out.append((m_new, l, acc))
            return tuple(out)

        init = tuple((jnp.full((1, tq), NEG_INF, F32), jnp.zeros((1, tq), F32), jnp.zeros((HEAD_DIM, tq), F32))
                     for _ in range(2))
        state = update(init, i, True)
        state = lax.fori_loop(0, i, lambda j, st: update(st, j, False), state)
        outs = [acc / l for _, l, acc in state]
    else:
        l_ref, acc_ref = scratch

        def add_blocks(js, diag_last, first):
            es = [[exponents(a, j, diag_last and idx == len(js) - 1) for a in range(2)]
                  for idx, j in enumerate(js)]
            ps = [[jnp.exp2(e) for e in pair] for pair in es]
            for a in range(2):
                l_new = sum(jnp.sum(pair[a].reshape(bk // 8, 8, tq), axis=0) for pair in ps)
                acc_new = sum(_dot(values(a, j), pair[a].astype(BF16)) for j, pair in zip(js, ps))
                if first:
                    l_ref[a] = l_new
                    acc_ref[a] = acc_new
                else:
                    l_ref[a] += l_new
                    acc_ref[a] += acc_new

        u_blocks = ATTN_BLOCKS_PER_STEP
        rem = _mod(i, u_blocks)
        for r in range(u_blocks):
            @pl.when(rem == r)
            def _():
                add_blocks([i - r + u for u in range(r)] + [i], True, True)

        def group(g, carry):
            add_blocks([g * u_blocks + u for u in range(u_blocks)], False, False)
            return carry

        lax.fori_loop(0, _div(i, u_blocks), group, 0)
        outs = [acc_ref[a] / jnp.sum(l_ref[a], axis=0, keepdims=True) for a in range(2)]
    o_ref[0] = jnp.concatenate(outs, axis=0).T.astype(o_ref.dtype)


def _out_kernel(oa_ref, ob_ref, za_ref, zb_ref, bg_ref, x_ref, gate_ref, wb_ref, wo_ref, y_ref):
    d = x_ref.shape[1]
    ga = (oa_ref[...].astype(F32) * za_ref[...].astype(F32)).astype(BF16)
    gb = (ob_ref[...].astype(F32) * zb_ref[...].astype(F32)).astype(BF16)
    ua = _dot(ga, wb_ref[0])
    ub = _dot(gb, wb_ref[1])
    merged = bg_ref[:, :d].astype(F32) * ua + bg_ref[:, d:].astype(F32) * ub
    y_ref[...] = x_ref[...] + gate_ref[0] * _dot(merged.astype(BF16), wo_ref[...])


def _mixer_out(oa, ob, za, zb, bg, x, gate, wb, wo, tm, tiles_per_mod):
    m, d = x.shape
    r = gate.shape[1]
    row = lambda width: pl.BlockSpec((tm, width), lambda i: (i, 0))
    return pl.pallas_call(
        _out_kernel,
        grid=(m // tm,),
        in_specs=[row(WIDTH), row(WIDTH), row(WIDTH), row(WIDTH), row(N_BRANCH * d), row(d),
                  pl.BlockSpec((1, r, d), lambda i: (_div(i, tiles_per_mod), 0, 0)),
                  pl.BlockSpec((N_BRANCH, WIDTH, d), lambda i: (0, 0, 0)),
                  pl.BlockSpec((d, d), lambda i: (0, 0))],
        out_specs=row(d),
        out_shape=jax.ShapeDtypeStruct((m, d), F32),
        compiler_params=_cparams(("parallel",), 48),
        name="mixer_out",
    )(oa, ob, za, zb, bg, x, gate, wb, wo)


def _block_diag_rows(q):
    t_new = q.shape[0]
    keep = (_iota((N_HEADS, WIDTH), 1) >> 6) == _iota((N_HEADS, WIDTH), 0)
    return jnp.concatenate([jnp.where(keep, q[t:t + 1, :], 0.0) for t in range(t_new)], axis=0)


def _collapse_rows(o64):
    rows = o64.shape[0]
    keep = (_iota((rows, WIDTH), 1) >> 6) == (_iota((rows, WIDTH), 0) & (N_HEADS - 1))
    om = jnp.where(keep, o64, 0.0)
    return jnp.concatenate([jnp.sum(om[t * N_HEADS:(t + 1) * N_HEADS], axis=0, keepdims=True)
                            for t in range(rows // N_HEADS)], axis=0)


def _pad_rows(a, rows):
    return jnp.concatenate([a, jnp.zeros((rows - a.shape[0], a.shape[1]), a.dtype)], axis=0)


def _new_token_init(qbd, kn_ref, vn_ref, bias, m_ref, l_ref, acc_ref):
    kn = _pad_rows(kn_ref[0], LANES).astype(BF16)
    vn = _pad_rows(vn_ref[0], LANES).astype(BF16)
    s = _dot_nt(qbd, kn)
    if bias is not None:
        s = s + bias
    s = jnp.where(_iota(s.shape, 1) <= (_iota(s.shape, 0) >> 3), s, NEG_INF)
    m = jnp.max(s, axis=1, keepdims=True)
    p = jnp.exp(s - m)
    m_ref[...] = m
    l_ref[...] = jnp.sum(p, axis=1, keepdims=True)
    acc_ref[...] = _dot(p.astype(BF16), vn)


def _online_update(s, vt, m_ref, l_ref, acc_ref):
    m_old = m_ref[...]
    m_new = jnp.maximum(m_old, jnp.max(s, axis=1, keepdims=True))
    alpha = jnp.exp(m_old - m_new)
    p = jnp.exp(s - m_new)
    m_ref[...] = m_new
    l_ref[...] = alpha * l_ref[...] + jnp.sum(p, axis=1, keepdims=True)
    acc_ref[...] = alpha * acc_ref[...] + _dot_nt(p.astype(BF16), vt)


def _pages(page_refs):
    return jnp.concatenate([r[0] for r in page_refs], axis=1)


def _moba_sample_body(group, steps, n_blocks, s, q_ref, kn_ref, vn_ref, k_refs, v_refs, o_ref, scratch):
    bm_ref, sc_ref, m_ref, l_ref, acc_ref = scratch
    ppb = MOBA_BLOCK // PAGE_SIZE
    keys = group * PAGE_SIZE
    q = q_ref[0]

    @pl.when(s == 0)
    def _():
        bm_ref[...] = jnp.zeros_like(bm_ref)

    @pl.when(s < steps)
    def _():
        lane_blk = _iota(bm_ref.shape, 1)
        for jb in range(group // ppb):
            tot = k_refs[ppb * jb][0]
            for e in range(1, ppb):
                tot = tot + k_refs[ppb * jb + e][0]
            mean = jnp.sum(tot, axis=1, keepdims=True) * (1.0 / MOBA_BLOCK)
            bm_ref[...] = jnp.where(lane_blk == s * (group // ppb) + jb, mean, bm_ref[...])
        qbd = _block_diag_rows(q * HEAD_DIM ** -0.5).astype(BF16)
        k0 = pl.multiple_of(s * keys, keys)
        sc_ref[:, pl.ds(k0, keys)] = _dot(qbd, _pages(k_refs).astype(BF16))

    @pl.when(s == steps - 1)
    def _():
        g = _dot(_block_diag_rows(q), bm_ref[...], HIGHEST)
        lane = _iota(g.shape, 1)
        lf = lane.astype(F32)
        valid = lane < n_blocks
        g = jnp.where(valid, g, NEG_INF)
        sel = jnp.zeros(g.shape, F32)
        for _ in range(min(MOBA_TOPK, n_blocks)):
            m = jnp.max(g, axis=1, keepdims=True)
            idx = jnp.min(jnp.where(g == m, lf, float(LANES)), axis=1, keepdims=True)
            pick = lf == idx
            sel = jnp.where(pick & valid, 1.0, sel)
            g = jnp.where(pick, NEG_INF, g)
        for j in range(n_blocks):
            keep = jnp.broadcast_to(sel[:, j:j + 1], (sel.shape[0], MOBA_BLOCK)) > 0.5
            cols = slice(j * MOBA_BLOCK, (j + 1) * MOBA_BLOCK)
            sc_ref[:, cols] = jnp.where(keep, sc_ref[:, cols], NEG_INF)
        qbd = _block_diag_rows(q * HEAD_DIM ** -0.5).astype(BF16)
        _new_token_init(qbd, kn_ref, vn_ref, None, m_ref, l_ref, acc_ref)

    @pl.when(s >= steps)
    def _():
        k0 = pl.multiple_of((s - steps) * keys, keys)
        _online_update(sc_ref[:, pl.ds(k0, keys)], _pages(v_refs).astype(BF16), m_ref, l_ref, acc_ref)

    @pl.when(s == 2 * steps - 1)
    def _():
        o_ref[0] = _collapse_rows(acc_ref[...] / l_ref[...])


def _sample_scratch(t_new):
    rows = t_new * N_HEADS
    return [pltpu.VMEM((rows, 1), F32), pltpu.VMEM((rows, 1), F32), pltpu.VMEM((rows, WIDTH), F32)]


def _suffix_kernel(n_pages, pt_ref, *refs):
    su_ref = refs[n_pages]
    o_ref = refs[n_pages + 1]
    x = jnp.concatenate([refs[g][0] for g in range(n_pages)], axis=0)
    later_in_page = (_iota((PAGE_SIZE, PAGE_SIZE), 0) > _iota((PAGE_SIZE, PAGE_SIZE), 1)).astype(BF16)
    in_page = sum(_dot(part, later_in_page) for part in _split_bf16(x, 3))
    totals = jnp.broadcast_to(jnp.sum(x, axis=1, keepdims=True), x.shape)
    later_pages = sum(_dot(su_ref[...], part) for part in _split_bf16(totals, 3))
    o_ref[0] = (in_page + later_pages).reshape(n_pages, N_HEADS, PAGE_SIZE)


def _fox_suffix(cache_logf_t, pt_flat, b, pages_per_seq):
    rows = pages_per_seq * N_HEADS
    r = jnp.arange(rows)
    su = ((r[None, :] > r[:, None]) & (r[None, :] % N_HEADS == r[:, None] % N_HEADS)).astype(BF16)
    grid_spec = pltpu.PrefetchScalarGridSpec(
        num_scalar_prefetch=1,
        grid=(b,),
        in_specs=[pl.BlockSpec((1, N_HEADS, PAGE_SIZE),
                               functools.partial(lambda g, bb, pt: (pt[bb * pages_per_seq + g], 0, 0), g))
                  for g in range(pages_per_seq)]
        + [pl.BlockSpec((rows, rows), lambda bb, pt: (0, 0))],
        out_specs=pl.BlockSpec((1, pages_per_seq, N_HEADS, PAGE_SIZE), lambda bb, pt: (bb, 0, 0, 0)),
    )
    return pl.pallas_call(
        functools.partial(_suffix_kernel, pages_per_seq),
        grid_spec=grid_spec,
        out_shape=jax.ShapeDtypeStruct((b, pages_per_seq, N_HEADS, PAGE_SIZE), F32),
        compiler_params=_cparams(("parallel",), 32),
        name="fox_sample_suffix",
    )(pt_flat, *([cache_logf_t] * pages_per_seq), su)


def _fox_sample_body(group, steps, s, q_ref, kn_ref, vn_ref, lfn_ref, sfx_ref, k_refs, v_refs, o_ref, scratch):
    m_ref, l_ref, acc_ref = scratch
    t_new = q_ref.shape[1]
    qbd = _block_diag_rows(q_ref[0] * HEAD_DIM ** -0.5).astype(BF16)

    @pl.when(s == 0)
    def _():
        upto = (_iota((LANES, LANES), 0) <= _iota((LANES, LANES), 1)).astype(F32)
        pre_t = _dot(lfn_ref[0], upto, HIGHEST)
        _new_token_init(qbd, kn_ref, vn_ref, -jnp.concatenate([pre_t] * t_new, axis=0), m_ref, l_ref, acc_ref)

    kt = _pages(k_refs).astype(BF16)
    vt = _pages(v_refs).astype(BF16)
    bias = jnp.concatenate([jnp.concatenate([sfx_ref[0, g]] * t_new, axis=0) for g in range(group)], axis=1)
    sc = _dot(qbd, kt) + bias
    _online_update(sc, vt, m_ref, l_ref, acc_ref)

    @pl.when(s == steps - 1)
    def _():
        o_ref[0] = _collapse_rows(acc_ref[...] / l_ref[...])


def _mixer_attn_kernel(fox, online, cfg, pt_ref, bound_ref, *refs):
    group, tiles = cfg["group"], cfg["tiles"]
    n_p_in = 5 if fox else 4
    q_ref, k_ref, vt_ref, b_ref = refs[:4]
    cq_ref = refs[4] if fox else None
    n_s_in = (5 if fox else 3) + 2 * group
    s_in = refs[n_p_in:n_p_in + n_s_in]
    op_ref, os_ref = refs[n_p_in + n_s_in:n_p_in + n_s_in + 2]
    scratch = refs[n_p_in + n_s_in + 2:]
    n_p_scratch = 0 if online else 2
    t = pl.program_id(0)

    @pl.when(t < cfg["prompt_steps"])
    def _():
        _prompt_attn_body(fox, online, _mod(_div(t, tiles), N_HEADS // 2), _mod(t, tiles), bound_ref,
                          q_ref, k_ref, vt_ref, b_ref, cq_ref, op_ref, scratch[:n_p_scratch])

    @pl.when(t < cfg["sample_steps"])
    def _():
        s = _mod(t, cfg["steps_per_sample"])
        pages = s_in[-2 * group:]
        if fox:
            _fox_sample_body(group, cfg["steps_per_sample"], s, *s_in[:5], pages[:group], pages[group:], os_ref,
                             scratch[n_p_scratch:])
        else:
            _moba_sample_body(group, cfg["steps_per_sample"] // 2, cfg["n_blocks"], s, *s_in[:3],
                              pages[:group], pages[group:], os_ref, scratch[n_p_scratch:])


def _mixer_attn(fox, online, bound, pt_flat, prompt_args, sample_args, cache_k, cache_v, pages_per_seq, group):
    q = prompt_args[0]
    n, t_len, _ = q.shape
    tq = MOBA_BLOCK
    tiles = t_len // tq
    b, t_new, _ = sample_args[0].shape
    rows = t_new * N_HEADS
    page_steps = pages_per_seq // group
    steps_per_sample = page_steps if fox else 2 * page_steps
    prompt_steps = n * (N_HEADS // 2) * tiles
    sample_steps = b * steps_per_sample
    total = max(prompt_steps, sample_steps)
    cfg = dict(group=group, tiles=tiles, prompt_steps=prompt_steps, sample_steps=sample_steps,
               steps_per_sample=steps_per_sample, n_blocks=pages_per_seq * PAGE_SIZE // MOBA_BLOCK)

    def p_idx(t):
        tp = t if prompt_steps == total else jnp.minimum(t, prompt_steps - 1)
        return _div(tp, tiles * (N_HEADS // 2)), _mod(_div(tp, tiles), N_HEADS // 2), _mod(tp, tiles)

    def s_idx(t):
        ts = t if sample_steps == total else jnp.minimum(t, sample_steps - 1)
        return _div(ts, steps_per_sample), _mod(ts, steps_per_sample)

    def pspec(shape, f):
        return pl.BlockSpec(shape, lambda t, pt: f(*p_idx(t)))

    def sspec(shape, f):
        return pl.BlockSpec(shape, lambda t, pt: f(*s_idx(t)))

    def page(g, phase):
        def index(t, pt):
            bb, s = s_idx(t)
            if fox:
                step = s
            else:
                step = jnp.minimum(s, page_steps - 1) if phase == 0 else jnp.maximum(s - page_steps, 0)
            return (pt[bb * pages_per_seq + step * group + g], 0, 0)
        return pl.BlockSpec((1, WIDTH, PAGE_SIZE), index)

    in_specs = [pl.BlockSpec(memory_space=pltpu.SMEM),
                pspec((1, tq, LANES), lambda bn, hp, i: (bn, i, hp)),
                pspec((1, t_len, LANES), lambda bn, hp, i: (bn, 0, hp)),
                pspec((1, LANES, t_len), lambda bn, hp, i: (bn, hp, 0))]
    if fox:
        in_specs += [pspec((1, t_len, 2 * LANES), lambda bn, hp, i: (bn, 0, hp)),
                     pspec((1, N_HEADS, tq), lambda bn, hp, i: (bn, 0, i))]
    else:
        nblk = t_len // MOBA_BLOCK
        in_specs.append(pspec((1, 1, 2 * nblk, tq), lambda bn, hp, i: (bn, i, hp, 0)))
    tok = sspec((1, t_new, WIDTH), lambda bb, s: (bb, 0, 0))
    in_specs += [tok, tok, tok]
    if fox:
        in_specs += [sspec((1, N_HEADS, LANES), lambda bb, s: (bb, 0, 0)),
                     sspec((1, group, N_HEADS, PAGE_SIZE), lambda bb, s: (bb, s, 0, 0))]
    in_specs += [page(g, 0) for g in range(group)] + [page(g, 1) for g in range(group)]
    scratch = [] if online else [pltpu.VMEM((2, 8, tq), F32), pltpu.VMEM((2, HEAD_DIM, tq), F32)]
    if not fox:
        scratch += [pltpu.VMEM((WIDTH, LANES), F32), pltpu.VMEM((rows, pages_per_seq * PAGE_SIZE), F32)]
    scratch += _sample_scratch(t_new)
    grid_spec = pltpu.PrefetchScalarGridSpec(
        num_scalar_prefetch=1,
        grid=(total,),
        in_specs=in_specs,
        out_specs=[pspec((1, tq, LANES), lambda bn, hp, i: (bn, i, hp)), tok],
        scratch_shapes=scratch,
    )
    name = ("fox" if fox else "moba") + "_attn" + ("_online" if online else "")
    return pl.pallas_call(
        functools.partial(_mixer_attn_kernel, fox, online, cfg),
        grid_spec=grid_spec,
        out_shape=[jax.ShapeDtypeStruct((n, t_len, WIDTH), BF16), jax.ShapeDtypeStruct((b, t_new, WIDTH), F32)],
        compiler_params=_cparams(("arbitrary",), 56),
        name=name,
    )(pt_flat, bound, *prompt_args, *sample_args, *([cache_k] * group), *([cache_v] * group))


def _mixer_attn_any_gain(fox, gq, gk, *args):
    slack = 1.02
    limit = 40.0
    bound = (HEAD_DIM ** 0.5) * jnp.max(jnp.abs(gq)) * jnp.max(jnp.abs(gk)) * slack
    bound2 = (bound * LOG2E).astype(F32).reshape(1, 1)
    return lax.cond(bound <= limit,
                    lambda: _mixer_attn(fox, False, bound2, *args),
                    lambda: _mixer_attn(fox, True, bound2, *args))


def _rope_tables(pos):
    half = HEAD_DIM // 2
    inv_freq = ROPE_THETA ** (-jnp.arange(half, dtype=F32) / half)
    ang = pos.astype(F32)[:, None] * inv_freq[None, :]
    cos = jnp.cos(ang)
    sin = jnp.sin(ang)
    cos_t = jnp.tile(cos, (1, LANES // half))
    sin_t = jnp.tile(jnp.concatenate([-sin, sin], axis=1), (1, LANES // HEAD_DIM))
    return cos_t, sin_t


def kernel(x_prompt, x_sample, cache_moba_k, cache_moba_v, cache_fox_k, cache_fox_v, cache_fox_logf, page_table, c_prompt, c_sample, norm_gain, w_ada, b_ada, w_in, b_forget, q_norm_a, k_norm_a, q_norm_b, k_norm_b, w_branch, w_out):
    n, t, d = x_prompt.shape
    b, t_new, _ = x_sample.shape
    depth = w_in.shape[0]
    n_pool = cache_moba_k.shape[1]
    pages = page_table.shape[1]
    past_len = pages * PAGE_SIZE
    assert t % MOBA_BLOCK == 0 and past_len % MOBA_BLOCK == 0 and (b * t_new) % ROW_TILE == 0
    assert ROW_TILE % t_new == 0 and t_new == N_HEADS
    group = min(16, pages)
    d_in = w_in.shape[2]
    w_cols = 12 * WIDTH + LANES

    pos_p = jnp.arange(t)
    pos_s = past_len + jnp.arange(t_new)
    cos_p, sin_p = _rope_tables(pos_p)
    cos_s, sin_s = (jnp.tile(a, (ROW_TILE // t_new, 1)) for a in _rope_tables(pos_s))
    lane_head = jnp.arange(WIDTH) // HEAD_DIM
    gsum = (lane_head[:, None] == lane_head[None, :]).astype(BF16)
    pt_flat = page_table.reshape(-1).astype(jnp.int32)

    yp = x_prompt.reshape(n * t, d)
    ys = x_sample.reshape(b * t_new, d)
    outs_p = [[] for _ in range(5)]
    outs_s = [[] for _ in range(5)]
    for l in range(depth):
        w_pad = jnp.pad(w_in[l], ((0, 0), (0, w_cols - d_in))).astype(BF16)
        bf_pad = jnp.pad(b_forget[l], (0, LANES - N_HEADS)).reshape(1, LANES)
        gains = [jnp.tile(g[l], N_HEADS).reshape(1, WIDTH) for g in (q_norm_a, k_norm_a, q_norm_b, k_norm_b)]
        gain = norm_gain[l].reshape(1, d)
        wb = w_branch[l].astype(BF16)
        wo = w_out[l].astype(BF16)

        mod = _ada_mod(c_prompt, w_ada[l], b_ada[l])
        shift, scale, gate_p = (mod[:, i * d:(i + 1) * d].reshape(n, 1, d) for i in range(3))
        (ka_t, va_t, kb_t, vb_t, logf_t, qa_s, qb_s, ka_b, kb_b, va_tb, vb_tb, za_p, zb_p, bg_p, selb, crep,
         cum_t) = _proj_in(True, yp, shift, scale, gain, w_pad, bf_pad, *gains, cos_p, sin_p, gsum, n, t)
        mod = _ada_mod(c_sample, w_ada[l], b_ada[l])
        rows = lambda a: jnp.repeat(a, t_new, axis=0).reshape(b * t_new // ROW_TILE, ROW_TILE, d)
        shift, scale, gate_s = (rows(mod[:, i * d:(i + 1) * d]) for i in range(3))
        (ka, va, kb, vb, logf, qa, qb, za_s, zb_s, bg_s) = _proj_in(
            False, ys, shift, scale, gain, w_pad, bf_pad, *gains, cos_s, sin_s, gsum, b, t_new)

        as3 = lambda a: a.reshape(n, t, a.shape[-1])
        tok3 = lambda a: a.reshape(b, t_new, WIDTH)
        page_t = lambda c: c[l].transpose(0, 2, 3, 1).reshape(n_pool, WIDTH, PAGE_SIZE)
        ck_a, cv_a, ck_b, cv_b = (page_t(c) for c in (cache_moba_k, cache_moba_v, cache_fox_k, cache_fox_v))
        sfx = _fox_suffix(cache_fox_logf[l].transpose(0, 2, 1), pt_flat, b, pages)
        lfn_t = jnp.pad(logf.reshape(b, t_new, N_HEADS).transpose(0, 2, 1), ((0, 0), (0, 0), (0, LANES - t_new)))
        oa_p, oa_s = _mixer_attn_any_gain(
            False, q_norm_a[l], k_norm_a[l], pt_flat, (as3(qa_s), as3(ka_b), va_tb, selb),
            (tok3(qa), tok3(ka), tok3(va)), ck_a, cv_a, pages, group)
        ob_p, ob_s = _mixer_attn_any_gain(
            True, q_norm_b[l], k_norm_b[l], pt_flat, (as3(qb_s), as3(kb_b), vb_tb, as3(crep), cum_t),
            (tok3(qb), tok3(kb), tok3(vb), lfn_t, sfx), ck_b, cv_b, pages, max(group // 2, 1))

        out_tile = OUT_ROW_TILE if t % OUT_ROW_TILE == 0 else ROW_TILE
        yp = _mixer_out(oa_p.reshape(n * t, WIDTH), ob_p.reshape(n * t, WIDTH), za_p, zb_p, bg_p, yp, gate_p, wb, wo,
                        out_tile, t // out_tile)
        ys = _mixer_out(oa_s.reshape(b * t_new, WIDTH).astype(BF16), ob_s.reshape(b * t_new, WIDTH).astype(BF16),
                        za_s, zb_s, bg_s, ys, gate_s, wb, wo, ROW_TILE, 1)
        for dst, a in zip(outs_p, (ka_t, va_t, kb_t, vb_t)):
            dst.append(a.reshape(n, N_HEADS, HEAD_DIM, t).transpose(0, 3, 1, 2))
        outs_p[4].append(logf_t.transpose(0, 2, 1))
        for dst, a in zip(outs_s, (ka, va, kb, vb)):
            dst.append(a.reshape(b, t_new, N_HEADS, HEAD_DIM))
        outs_s[4].append(logf.reshape(b, t_new, N_HEADS))

    return (yp.reshape(n, t, d), ys.reshape(b, t_new, d),
            *(jnp.stack(o) for o in outs_p), *(jnp.stack(o) for o in outs_s))
```

```python
import functools

import jax
import jax.numpy as jnp
from jax import lax
from jax.experimental import pallas as pl
from jax.experimental.pallas import tpu as pltpu

F32 = jnp.float32
BF16 = jnp.bfloat16
HIGHEST = lax.Precision.HIGHEST
NEG_INF = float("-inf")

HEAD_DIM = 64
N_HEADS = 8
WIDTH = N_HEADS * HEAD_DIM
N_BRANCH = 2
MOBA_BLOCK = 256
MOBA_TOPK = 3
PAGE_SIZE = 128
ROPE_THETA = 10000.0
EPS = 1e-6
LOG2E = 1.4426950408889634
LANES = 128
ROW_TILE = 256
OUT_ROW_TILE = 512
ATTN_BLOCKS_PER_STEP = 16
MiB = 1024 * 1024


def _cparams(dims, vmem_mib):
    return pltpu.CompilerParams(dimension_semantics=dims, vmem_limit_bytes=vmem_mib * MiB)


def _dot(a, b, precision=None):
    return jnp.dot(a, b, preferred_element_type=F32, precision=precision)


def _dot_nt(a, b, precision=None):
    return lax.dot_general(a, b, (((1,), (1,)), ((), ())), preferred_element_type=F32, precision=precision)


def _iota(shape, dim):
    return lax.broadcasted_iota(jnp.int32, shape, dim)


def _div(x, d):
    if d & (d - 1) == 0:
        return lax.shift_right_logical(x, jnp.int32(d.bit_length() - 1))
    return x // d


def _mod(x, d):
    if d & (d - 1) == 0:
        return x & (d - 1)
    return x % d


def _split_bf16(x, terms):
    parts = []
    for _ in range(terms):
        p = x.astype(BF16)
        parts.append(p)
        x = x - p.astype(F32)
    return parts


def _ada_kernel(c_ref, w_ref, b_ref, o_ref):
    c = c_ref[...]
    a = c * jax.nn.sigmoid(c)
    o_ref[...] = _dot(a, w_ref[...], HIGHEST) + b_ref[...]


def _ada_mod(c, w_ada, b_ada):
    n, d = c.shape
    d3 = w_ada.shape[1]
    tn = 512
    return pl.pallas_call(
        _ada_kernel,
        grid=(d3 // tn,),
        in_specs=[pl.BlockSpec((n, d), lambda j: (0, 0)),
                  pl.BlockSpec((d, tn), lambda j: (0, j)),
                  pl.BlockSpec((1, tn), lambda j: (0, j))],
        out_specs=pl.BlockSpec((n, tn), lambda j: (0, j)),
        out_shape=jax.ShapeDtypeStruct((n, d3), F32),
        compiler_params=_cparams(("parallel",), 32),
        name="ada_mod",
    )(c, w_ada, b_ada.reshape(1, d3))


def _log_sigmoid(v):
    return jnp.minimum(v, 0.0) - jnp.log1p(jnp.exp(-jnp.abs(v)))


def _proj_kernel(prompt, tiles_per_seq, subs, *refs):
    t = ROW_TILE
    rows, lanes, blk = "rows", "lanes", "blk"
    kinds = {0: rows, 10: rows, 11: rows}
    if prompt:
        out_kinds = [lanes] * 5 + [rows] * 4 + [lanes] * 2 + [rows] * 3 + [blk, rows, lanes]
        kinds.update({13 + k: v for k, v in enumerate(out_kinds)})
    for sub in range(subs):
        views = []
        for k, r in enumerate(refs):
            kind = kinds.get(k) if subs > 1 else None
            if kind == rows:
                r = r.at[sub * t:(sub + 1) * t]
            elif kind == lanes:
                r = r.at[:, :, sub * t:(sub + 1) * t]
            elif kind == blk:
                r = r.at[:, sub:sub + 1]
            views.append(r)
        _proj_tile(prompt, _mod(pl.program_id(0) * subs + sub, tiles_per_seq), *views)


def _proj_tile(prompt, it, *refs):
    (x_ref, shift_ref, scale_ref, gain_ref, w_ref, bf_ref, gqa_ref, gka_ref, gqb_ref, gkb_ref,
     cos_ref, sin_ref, gsum_ref) = refs[:13]
    if prompt:
        (kat_ref, vat_ref, kbt_ref, vbt_ref, logft_ref, qa_ref, qb_ref, kab_ref, kbb_ref, vatb_ref, vbtb_ref,
         za_ref, zb_ref, bg_ref, selb_ref, crep_ref, cumt_ref, means_ref, carry_ref) = refs[13:]
    else:
        (ka_ref, va_ref, kb_ref, vb_ref, logf_ref, qa_ref, qb_ref, za_ref, zb_ref, bg_ref) = refs[13:]
    tm = x_ref.shape[0]

    x = x_ref[...]
    ms = jnp.mean(x * x, axis=-1, keepdims=True)
    h = x * lax.rsqrt(ms + EPS) * gain_ref[...]
    h = h * (1.0 + scale_ref[0]) + shift_ref[0]
    hb = h.astype(BF16)

    def seg(c0, width=WIDTH):
        return _dot(hb, w_ref[:, c0:c0 + width])

    def head_norm(z, g_ref):
        ss = _dot((z * z).astype(BF16), gsum_ref[...])
        return z * lax.rsqrt(ss * (1.0 / HEAD_DIM) + EPS) * g_ref[...]

    cosf = jnp.concatenate([cos_ref[...]] * (WIDTH // LANES), axis=1)
    sinf = jnp.concatenate([sin_ref[...]] * (WIDTH // LANES), axis=1)
    first_half = (_iota((tm, WIDTH), 1) & (HEAD_DIM - 1)) < (HEAD_DIM // 2)

    def rope(y):
        partner = jnp.where(first_half, pltpu.roll(y, WIDTH - HEAD_DIM // 2, 1), pltpu.roll(y, HEAD_DIM // 2, 1))
        return y * cosf + partner * sinf

    lf = _log_sigmoid(seg(12 * WIDTH, LANES) + bf_ref[...])
    if prompt:
        nblk = means_ref.shape[0]

        means_ref[...] = jnp.where(it == 0, 0.0, means_ref[...])
        carry = jnp.where(it == 0, 0.0, carry_ref[...])

        tri = (_iota((tm, tm), 1) <= _iota((tm, tm), 0)).astype(BF16)
        cum = sum(_dot(tri, part) for part in _split_bf16(lf, 3)) + carry
        carry_ref[...] = cum[tm - 1:tm, :]
        cum2 = cum * LOG2E
        cumt_ref[0] = cum2.T[:N_HEADS, :]
        for hh in range(N_HEADS):
            crep_ref[:, hh * LANES:(hh + 1) * LANES] = jnp.broadcast_to(cum2[:, hh:hh + 1], (tm, LANES))
        logft_ref[0] = lf.T[:N_HEADS, :]

    qa = rope(head_norm(seg(0), gqa_ref))
    ka = rope(head_norm(seg(WIDTH), gka_ref))
    if prompt:
        mt = means_ref[...]
        mbd = jnp.concatenate([mt] * N_HEADS, axis=0)
        rr = _iota(mbd.shape, 0)
        ll = _iota(mbd.shape, 1)
        head_of_row = jnp.zeros(mbd.shape, jnp.int32)
        for hh in range(1, N_HEADS):
            head_of_row = head_of_row + (rr >= hh * nblk).astype(jnp.int32)
        mbd = jnp.where(head_of_row == (ll >> 6), mbd, 0.0)
        m_hi, m_lo = _split_bf16(mbd, 2)
        q_hi, q_lo = _split_bf16(qa, 2)
        st = _dot_nt(m_hi, q_hi) + (_dot_nt(m_hi, q_lo) + _dot_nt(m_lo, q_hi))
        jj = _iota((nblk, tm), 0)
        jf = jj.astype(F32)
        valid = jj < it
        for hh in range(N_HEADS):
            s = jnp.where(valid, st[hh * nblk:(hh + 1) * nblk, :], NEG_INF)
            sel = jnp.zeros((nblk, tm), F32)
            for _ in range(MOBA_TOPK):
                m = jnp.max(s, axis=0, keepdims=True)
                idx = jnp.min(jnp.where(s == m, jf, float(nblk)), axis=0, keepdims=True)
                pick = jf == idx
                sel = jnp.where(pick & valid, 1.0, sel)
                s = jnp.where(pick, NEG_INF, s)
            selb_ref[0, 0, hh * nblk:(hh + 1) * nblk, :] = jnp.where(sel > 0.5, 0.0, NEG_INF)
        means_ref[pl.ds(it, 1), :] = jnp.mean(ka, axis=0, keepdims=True)

    va = seg(2 * WIDTH)
    za = seg(3 * WIDTH)
    za_ref[...] = (za * jax.nn.sigmoid(za)).astype(BF16)
    qb = head_norm(seg(4 * WIDTH), gqb_ref)
    kb = head_norm(seg(5 * WIDTH), gkb_ref)
    vb = seg(6 * WIDTH)
    zb = seg(7 * WIDTH)
    zb_ref[...] = (zb * jax.nn.sigmoid(zb)).astype(BF16)
    for s in range(2 * N_BRANCH):
        g = seg(8 * WIDTH + s * WIDTH)
        bg_ref[:, s * WIDTH:(s + 1) * WIDTH] = jax.nn.sigmoid(g).astype(BF16)

    if not prompt:
        ka_ref[...] = ka
        va_ref[...] = va
        kb_ref[...] = kb
        vb_ref[...] = vb
        logf_ref[...] = lf[:, :N_HEADS]
        qa_ref[...] = qa
        qb_ref[...] = qb
        return

    scale = HEAD_DIM ** -0.5 * LOG2E
    qa_ref[...] = (qa * scale).astype(BF16)
    qb_ref[...] = (qb * scale).astype(BF16)
    kab_ref[...] = ka.astype(BF16)
    kbb_ref[...] = kb.astype(BF16)
    kat_ref[0] = ka.T
    kbt_ref[0] = kb.T
    va_t = va.T
    vb_t = vb.T
    vat_ref[0] = va_t
    vbt_ref[0] = vb_t
    vatb_ref[0] = va_t.astype(BF16)
    vbtb_ref[0] = vb_t.astype(BF16)


def _proj_in(prompt, x, shift, scale, gain, w_pad, bf_pad, gqa, gka, gqb, gkb, cos_t, sin_t, gsum, n_seq, seq_len):
    m, d = x.shape
    subs = 2 if prompt and (seq_len // ROW_TILE) % 2 == 0 else 1
    tm = ROW_TILE * subs
    n_tiles = m // tm
    if prompt:
        tiles_per_seq = seq_len // tm
        tiles_per_mod = tiles_per_seq
    else:
        tiles_per_seq = 1
        tiles_per_mod = 1
    r = shift.shape[1]
    tab_tiles = cos_t.shape[0] // tm
    wp = w_pad.shape[1]
    row = lambda width: pl.BlockSpec((tm, width), lambda i: (i, 0))
    const = lambda shape: pl.BlockSpec(shape, lambda i: (0,) * len(shape))
    in_specs = [
        row(d),
        pl.BlockSpec((1, r, d), lambda i: (_div(i, tiles_per_mod), 0, 0)),
        pl.BlockSpec((1, r, d), lambda i: (_div(i, tiles_per_mod), 0, 0)),
        const((1, d)),
        const((d, wp)),
        const((1, LANES)),
        const((1, WIDTH)), const((1, WIDTH)), const((1, WIDTH)), const((1, WIDTH)),
        pl.BlockSpec((tm, LANES), lambda i: (_mod(i, tab_tiles), 0)),
        pl.BlockSpec((tm, LANES), lambda i: (_mod(i, tab_tiles), 0)),
        const((WIDTH, WIDTH)),
    ]
    f32_tok = jax.ShapeDtypeStruct((m, WIDTH), F32)
    bf_tok = jax.ShapeDtypeStruct((m, WIDTH), BF16)
    scratch = []
    if prompt:
        nblk = seq_len // MOBA_BLOCK
        vt = jax.ShapeDtypeStruct((n_seq, WIDTH, seq_len), BF16)
        ft = jax.ShapeDtypeStruct((n_seq, WIDTH, seq_len), F32)
        vt_spec = pl.BlockSpec((1, WIDTH, tm), lambda i: (_div(i, tiles_per_seq), 0, _mod(i, tiles_per_seq)))
        out_shape = [ft, ft, ft, ft, jax.ShapeDtypeStruct((n_seq, N_HEADS, seq_len), F32)]
        out_specs = [vt_spec, vt_spec, vt_spec, vt_spec,
                     pl.BlockSpec((1, N_HEADS, tm), lambda i: (_div(i, tiles_per_seq), 0, _mod(i, tiles_per_seq)))]
        out_shape += [bf_tok, bf_tok, bf_tok, bf_tok, vt, vt, bf_tok, bf_tok,
                      jax.ShapeDtypeStruct((m, 2 * N_BRANCH * WIDTH), BF16),
                      jax.ShapeDtypeStruct((n_seq, nblk, N_HEADS * nblk, ROW_TILE), F32),
                      jax.ShapeDtypeStruct((m, N_HEADS * LANES), F32),
                      jax.ShapeDtypeStruct((n_seq, N_HEADS, seq_len), F32)]
        out_specs += [row(WIDTH), row(WIDTH), row(WIDTH), row(WIDTH), vt_spec, vt_spec, row(WIDTH), row(WIDTH),
                      row(2 * N_BRANCH * WIDTH),
                      pl.BlockSpec((1, subs, N_HEADS * nblk, ROW_TILE), lambda i: (_div(i, tiles_per_seq), _mod(i, tiles_per_seq), 0, 0)),
                      row(N_HEADS * LANES),
                      pl.BlockSpec((1, N_HEADS, tm), lambda i: (_div(i, tiles_per_seq), 0, _mod(i, tiles_per_seq)))]
        scratch = [pltpu.VMEM((nblk, WIDTH), F32), pltpu.VMEM((1, LANES), F32)]
    else:
        out_shape = [f32_tok, f32_tok, f32_tok, f32_tok, jax.ShapeDtypeStruct((m, N_HEADS), F32),
                     f32_tok, f32_tok, bf_tok, bf_tok, jax.ShapeDtypeStruct((m, 2 * N_BRANCH * WIDTH), BF16)]
        out_specs = [row(WIDTH), row(WIDTH), row(WIDTH), row(WIDTH), row(N_HEADS),
                     row(WIDTH), row(WIDTH), row(WIDTH), row(WIDTH), row(2 * N_BRANCH * WIDTH)]
    return pl.pallas_call(
        functools.partial(_proj_kernel, prompt, (seq_len // ROW_TILE) if prompt else 1, subs),
        grid=(n_tiles,),
        in_specs=in_specs,
        out_specs=out_specs,
        out_shape=out_shape,
        scratch_shapes=scratch,
        compiler_params=_cparams(("arbitrary",), 56),
        name="proj_in_prompt" if prompt else "proj_in_sample",
    )(x, shift, scale, gain, w_pad, bf_pad, gqa, gka, gqb, gkb, cos_t, sin_t, gsum)


def _prompt_attn_body(fox, online, hp, i, bound_ref, q_ref, k_ref, vt_ref, b_ref, cq_ref, o_ref, scratch):
    tq = q_ref.shape[1]
    bk = tq
    q = q_ref[0]
    lane = _iota(q.shape, 1)
    zero = jnp.zeros_like(q)
    qm = (jnp.where(lane < HEAD_DIM, q, zero), jnp.where(lane >= HEAD_DIM, q, zero))
    nblk = b_ref.shape[2] // 2 if not fox else 0
    bound = 0.0 if online else bound_ref[0, 0]
    if fox and not online:
        rq = [cq_ref[0, pl.ds(2 * hp + a, 1), :] - bound for a in range(2)]

    def exponents(a, j, diag):
        j0 = pl.multiple_of(j * bk, bk)
        s = _dot_nt(k_ref[0, pl.ds(j0, bk), :], qm[a])
        if fox:
            c = b_ref[0, pl.ds(j0, bk), a * LANES:(a + 1) * LANES]
            if not online:
                s = s + rq[a]
            s = s - jnp.concatenate([c] * (tq // LANES), axis=1)
        elif not diag:
            s = s + (b_ref[0, 0, pl.ds(a * nblk + j, 1), :] - bound)
        elif not online:
            s = s - bound
        if diag:
            s = jnp.where(_iota(s.shape, 0) <= _iota(s.shape, 1), s, NEG_INF)
        return s

    def values(a, j):
        j0 = pl.multiple_of(j * bk, bk)
        return vt_ref[0, a * HEAD_DIM:(a + 1) * HEAD_DIM, pl.ds(j0, bk)]

    if online:
        def update(state, j, diag):
            out = []
            for a in range(2):
                m, l, acc = state[a]
                e = exponents(a, j, diag)
                m_new = jnp.maximum(m, jnp.max(e, axis=0, keepdims=True))
                alpha = jnp.exp2(m - m_new)
                p = jnp.exp2(e - m_new)
                l = alpha * l + jnp.sum(p, axis=0, keepdims=True)
                acc = alpha * acc + _dot(values(a, j), p.astype(BF16))
                out.append((m_new, l, acc))
            return tuple(out)

        init = tuple((jnp.full((1, tq), NEG_INF, F32), jnp.zeros((1, tq), F32), jnp.zeros((HEAD_DIM, tq), F32))
                     for _ in range(2))
        state = update(init, i, True)
        state = lax.fori_loop(0, i, lambda j, st: update(st, j, False), state)
        outs = [acc / l for _, l, acc in state]
    else:
        l_ref, acc_ref = scratch

        def add_blocks(js, diag_last, first):
            es = [[exponents(a, j, diag_last and idx == len(js) - 1) for a in range(2)]
                  for idx, j in enumerate(js)]
            ps = [[jnp.exp2(e) for e in pair] for pair in es]
            for a in range(2):
                l_new = sum(jnp.sum(pair[a].reshape(bk // 8, 8, tq), axis=0) for pair in ps)
                acc_new = sum(_dot(values(a, j), pair[a].astype(BF16)) for j, pair in zip(js, ps))
                if first:
                    l_ref[a] = l_new
                    acc_ref[a] = acc_new
                else:
                    l_ref[a] += l_new
                    acc_ref[a] += acc_new

        u_blocks = ATTN_BLOCKS_PER_STEP
        rem = _mod(i, u_blocks)
        for r in range(u_blocks):
            @pl.when(rem == r)
            def _():
                add_blocks([i - r + u for u in range(r)] + [i], True, True)

        def group(g, carry):
            add_blocks([g * u_blocks + u for u in range(u_blocks)], False, False)
            return carry

        lax.fori_loop(0, _div(i, u_blocks), group, 0)
        outs = [acc_ref[a] / jnp.sum(l_ref[a], axis=0, keepdims=True) for a in range(2)]
    o_ref[0] = jnp.concatenate(outs, axis=0).T.astype(o_ref.dtype)


def _out_kernel(oa_ref, ob_ref, za_ref, zb_ref, bg_ref, x_ref, gate_ref, wb_ref, wo_ref, y_ref):
    d = x_ref.shape[1]
    ga = (oa_ref[...].astype(F32) * za_ref[...].astype(F32)).astype(BF16)
    gb = (ob_ref[...].astype(F32) * zb_ref[...].astype(F32)).astype(BF16)
    ua = _dot(ga, wb_ref[0])
    ub = _dot(gb, wb_ref[1])
    merged = bg_ref[:, :d].astype(F32) * ua + bg_ref[:, d:].astype(F32) * ub
    y_ref[...] = x_ref[...] + gate_ref[0] * _dot(merged.astype(BF16), wo_ref[...])


def _mixer_out(oa, ob, za, zb, bg, x, gate, wb, wo, tm, tiles_per_mod):
    m, d = x.shape
    r = gate.shape[1]
    row = lambda width: pl.BlockSpec((tm, width), lambda i: (i, 0))
    return pl.pallas_call(
        _out_kernel,
        grid=(m // tm,),
        in_specs=[row(WIDTH), row(WIDTH), row(WIDTH), row(WIDTH), row(N_BRANCH * d), row(d),
                  pl.BlockSpec((1, r, d), lambda i: (_div(i, tiles_per_mod), 0, 0)),
                  pl.BlockSpec((N_BRANCH, WIDTH, d), lambda i: (0, 0, 0)),
                  pl.BlockSpec((d, d), lambda i: (0, 0))],
        out_specs=row(d),
        out_shape=jax.ShapeDtypeStruct((m, d), F32),
        compiler_params=_cparams(("parallel",), 48),
        name="mixer_out",
    )(oa, ob, za, zb, bg, x, gate, wb, wo)


def _block_diag_rows(q):
    t_new = q.shape[0]
    keep = (_iota((N_HEADS, WIDTH), 1) >> 6) == _iota((N_HEADS, WIDTH), 0)
    return jnp.concatenate([jnp.where(keep, q[t:t + 1, :], 0.0) for t in range(t_new)], axis=0)


def _collapse_rows(o64):
    rows = o64.shape[0]
    keep = (_iota((rows, WIDTH), 1) >> 6) == (_iota((rows, WIDTH), 0) & (N_HEADS - 1))
    om = jnp.where(keep, o64, 0.0)
    return jnp.concatenate([jnp.sum(om[t * N_HEADS:(t + 1) * N_HEADS], axis=0, keepdims=True)
                            for t in range(rows // N_HEADS)], axis=0)


def _pad_rows(a, rows):
    return jnp.concatenate([a, jnp.zeros((rows - a.shape[0], a.shape[1]), a.dtype)], axis=0)


def _new_token_init(qbd, kn_ref, vn_ref, bias, m_ref, l_ref, acc_ref):
    kn = _pad_rows(kn_ref[0], LANES).astype(BF16)
    vn = _pad_rows(vn_ref[0], LANES).astype(BF16)
    s = _dot_nt(qbd, kn)
    if bias is not None:
        s = s + bias
    s = jnp.where(_iota(s.shape, 1) <= (_iota(s.shape, 0) >> 3), s, NEG_INF)
    m = jnp.max(s, axis=1, keepdims=True)
    p = jnp.exp(s - m)
    m_ref[...] = m
    l_ref[...] = jnp.sum(p, axis=1, keepdims=True)
    acc_ref[...] = _dot(p.astype(BF16), vn)


def _online_update(s, vt, m_ref, l_ref, acc_ref):
    m_old = m_ref[...]
    m_new = jnp.maximum(m_old, jnp.max(s, axis=1, keepdims=True))
    alpha = jnp.exp(m_old - m_new)
    p = jnp.exp(s - m_new)
    m_ref[...] = m_new
    l_ref[...] = alpha * l_ref[...] + jnp.sum(p, axis=1, keepdims=True)
    acc_ref[...] = alpha * acc_ref[...] + _dot_nt(p.astype(BF16), vt)


def _pages(page_refs):
    return jnp.concatenate([r[0] for r in page_refs], axis=1)


def _moba_sample_body(group, steps, n_blocks, s, q_ref, kn_ref, vn_ref, k_refs, v_refs, o_ref, scratch):
    bm_ref, sc_ref, m_ref, l_ref, acc_ref = scratch
    ppb = MOBA_BLOCK // PAGE_SIZE
    keys = group * PAGE_SIZE
    q = q_ref[0]

    @pl.when(s == 0)
    def _():
        bm_ref[...] = jnp.zeros_like(bm_ref)

    @pl.when(s < steps)
    def _():
        lane_blk = _iota(bm_ref.shape, 1)
        for jb in range(group // ppb):
            tot = k_refs[ppb * jb][0]
            for e in range(1, ppb):
                tot = tot + k_refs[ppb * jb + e][0]
            mean = jnp.sum(tot, axis=1, keepdims=True) * (1.0 / MOBA_BLOCK)
            bm_ref[...] = jnp.where(lane_blk == s * (group // ppb) + jb, mean, bm_ref[...])
        qbd = _block_diag_rows(q * HEAD_DIM ** -0.5).astype(BF16)
        k0 = pl.multiple_of(s * keys, keys)
        sc_ref[:, pl.ds(k0, keys)] = _dot(qbd, _pages(k_refs).astype(BF16))

    @pl.when(s == steps - 1)
    def _():
        g = _dot(_block_diag_rows(q), bm_ref[...], HIGHEST)
        lane = _iota(g.shape, 1)
        lf = lane.astype(F32)
        valid = lane < n_blocks
        g = jnp.where(valid, g, NEG_INF)
        sel = jnp.zeros(g.shape, F32)
        for _ in range(min(MOBA_TOPK, n_blocks)):
            m = jnp.max(g, axis=1, keepdims=True)
            idx = jnp.min(jnp.where(g == m, lf, float(LANES)), axis=1, keepdims=True)
            pick = lf == idx
            sel = jnp.where(pick & valid, 1.0, sel)
            g = jnp.where(pick, NEG_INF, g)
        for j in range(n_blocks):
            keep = jnp.broadcast_to(sel[:, j:j + 1], (sel.shape[0], MOBA_BLOCK)) > 0.5
            cols = slice(j * MOBA_BLOCK, (j + 1) * MOBA_BLOCK)
            sc_ref[:, cols] = jnp.where(keep, sc_ref[:, cols], NEG_INF)
        qbd = _block_diag_rows(q * HEAD_DIM ** -0.5).astype(BF16)
        _new_token_init(qbd, kn_ref, vn_ref, None, m_ref, l_ref, acc_ref)

    @pl.when(s >= steps)
    def _():
        k0 = pl.multiple_of((s - steps) * keys, keys)
        _online_update(sc_ref[:, pl.ds(k0, keys)], _pages(v_refs).astype(BF16), m_ref, l_ref, acc_ref)

    @pl.when(s == 2 * steps - 1)
    def _():
        o_ref[0] = _collapse_rows(acc_ref[...] / l_ref[...])


def _sample_scratch(t_new):
    rows = t_new * N_HEADS
    return [pltpu.VMEM((rows, 1), F32), pltpu.VMEM((rows, 1), F32), pltpu.VMEM((rows, WIDTH), F32)]


def _suffix_kernel(n_pages, pt_ref, *refs):
    su_ref = refs[n_pages]
    o_ref = refs[n_pages + 1]
    x = jnp.concatenate([refs[g][0] for g in range(n_pages)], axis=0)
    later_in_page = (_iota((PAGE_SIZE, PAGE_SIZE), 0) > _iota((PAGE_SIZE, PAGE_SIZE), 1)).astype(BF16)
    in_page = sum(_dot(part, later_in_page) for part in _split_bf16(x, 3))
    totals = jnp.broadcast_to(jnp.sum(x, axis=1, keepdims=True), x.shape)
    later_pages = sum(_dot(su_ref[...], part) for part in _split_bf16(totals, 3))
    o_ref[0] = (in_page + later_pages).reshape(n_pages, N_HEADS, PAGE_SIZE)


def _fox_suffix(cache_logf_t, pt_flat, b, pages_per_seq):
    rows = pages_per_seq * N_HEADS
    r = jnp.arange(rows)
    su = ((r[None, :] > r[:, None]) & (r[None, :] % N_HEADS == r[:, None] % N_HEADS)).astype(BF16)
    grid_spec = pltpu.PrefetchScalarGridSpec(
        num_scalar_prefetch=1,
        grid=(b,),
        in_specs=[pl.BlockSpec((1, N_HEADS, PAGE_SIZE),
                               functools.partial(lambda g, bb, pt: (pt[bb * pages_per_seq + g], 0, 0), g))
                  for g in range(pages_per_seq)]
        + [pl.BlockSpec((rows, rows), lambda bb, pt: (0, 0))],
        out_specs=pl.BlockSpec((1, pages_per_seq, N_HEADS, PAGE_SIZE), lambda bb, pt: (bb, 0, 0, 0)),
    )
    return pl.pallas_call(
        functools.partial(_suffix_kernel, pages_per_seq),
        grid_spec=grid_spec,
        out_shape=jax.ShapeDtypeStruct((b, pages_per_seq, N_HEADS, PAGE_SIZE), F32),
        compiler_params=_cparams(("parallel",), 32),
        name="fox_sample_suffix",
    )(pt_flat, *([cache_logf_t] * pages_per_seq), su)


def _fox_sample_body(group, steps, s, q_ref, kn_ref, vn_ref, lfn_ref, sfx_ref, k_refs, v_refs, o_ref, scratch):
    m_ref, l_ref, acc_ref = scratch
    t_new = q_ref.shape[1]
    qbd = _block_diag_rows(q_ref[0] * HEAD_DIM ** -0.5).astype(BF16)

    @pl.when(s == 0)
    def _():
        upto = (_iota((LANES, LANES), 0) <= _iota((LANES, LANES), 1)).astype(F32)
        pre_t = _dot(lfn_ref[0], upto, HIGHEST)
        _new_token_init(qbd, kn_ref, vn_ref, -jnp.concatenate([pre_t] * t_new, axis=0), m_ref, l_ref, acc_ref)

    kt = _pages(k_refs).astype(BF16)
    vt = _pages(v_refs).astype(BF16)
    bias = jnp.concatenate([jnp.concatenate([sfx_ref[0, g]] * t_new, axis=0) for g in range(group)], axis=1)
    sc = _dot(qbd, kt) + bias
    _online_update(sc, vt, m_ref, l_ref, acc_ref)

    @pl.when(s == steps - 1)
    def _():
        o_ref[0] = _collapse_rows(acc_ref[...] / l_ref[...])


def _mixer_attn_kernel(fox, online, cfg, pt_ref, bound_ref, *refs):
    group, tiles = cfg["group"], cfg["tiles"]
    n_p_in = 5 if fox else 4
    q_ref, k_ref, vt_ref, b_ref = refs[:4]
    cq_ref = refs[4] if fox else None
    n_s_in = (5 if fox else 3) + 2 * group
    s_in = refs[n_p_in:n_p_in + n_s_in]
    op_ref, os_ref = refs[n_p_in + n_s_in:n_p_in + n_s_in + 2]
    scratch = refs[n_p_in + n_s_in + 2:]
    n_p_scratch = 0 if online else 2
    t = pl.program_id(0)

    @pl.when(t < cfg["prompt_steps"])
    def _():
        _prompt_attn_body(fox, online, _mod(_div(t, tiles), N_HEADS // 2), _mod(t, tiles), bound_ref,
                          q_ref, k_ref, vt_ref, b_ref, cq_ref, op_ref, scratch[:n_p_scratch])

    @pl.when(t < cfg["sample_steps"])
    def _():
        s = _mod(t, cfg["steps_per_sample"])
        pages = s_in[-2 * group:]
        if fox:
            _fox_sample_body(group, cfg["steps_per_sample"], s, *s_in[:5], pages[:group], pages[group:], os_ref,
                             scratch[n_p_scratch:])
        else:
            _moba_sample_body(group, cfg["steps_per_sample"] // 2, cfg["n_blocks"], s, *s_in[:3],
                              pages[:group], pages[group:], os_ref, scratch[n_p_scratch:])


def _mixer_attn(fox, online, bound, pt_flat, prompt_args, sample_args, cache_k, cache_v, pages_per_seq, group):
    q = prompt_args[0]
    n, t_len, _ = q.shape
    tq = MOBA_BLOCK
    tiles = t_len // tq
    b, t_new, _ = sample_args[0].shape
    rows = t_new * N_HEADS
    page_steps = pages_per_seq // group
    steps_per_sample = page_steps if fox else 2 * page_steps
    prompt_steps = n * (N_HEADS // 2) * tiles
    sample_steps = b * steps_per_sample
    total = max(prompt_steps, sample_steps)
    cfg = dict(group=group, tiles=tiles, prompt_steps=prompt_steps, sample_steps=sample_steps,
               steps_per_sample=steps_per_sample, n_blocks=pages_per_seq * PAGE_SIZE // MOBA_BLOCK)

    def p_idx(t):
        tp = t if prompt_steps == total else jnp.minimum(t, prompt_steps - 1)
        return _div(tp, tiles * (N_HEADS // 2)), _mod(_div(tp, tiles), N_HEADS // 2), _mod(tp, tiles)

    def s_idx(t):
        ts = t if sample_steps == total else jnp.minimum(t, sample_steps - 1)
        return _div(ts, steps_per_sample), _mod(ts, steps_per_sample)

    def pspec(shape, f):
        return pl.BlockSpec(shape, lambda t, pt: f(*p_idx(t)))

    def sspec(shape, f):
        return pl.BlockSpec(shape, lambda t, pt: f(*s_idx(t)))

    def page(g, phase):
        def index(t, pt):
            bb, s = s_idx(t)
            if fox:
                step = s
            else:
                step = jnp.minimum(s, page_steps - 1) if phase == 0 else jnp.maximum(s - page_steps, 0)
            return (pt[bb * pages_per_seq + step * group + g], 0, 0)
        return pl.BlockSpec((1, WIDTH, PAGE_SIZE), index)

    in_specs = [pl.BlockSpec(memory_space=pltpu.SMEM),
                pspec((1, tq, LANES), lambda bn, hp, i: (bn, i, hp)),
                pspec((1, t_len, LANES), lambda bn, hp, i: (bn, 0, hp)),
                pspec((1, LANES, t_len), lambda bn, hp, i: (bn, hp, 0))]
    if fox:
        in_specs += [pspec((1, t_len, 2 * LANES), lambda bn, hp, i: (bn, 0, hp)),
                     pspec((1, N_HEADS, tq), lambda bn, hp, i: (bn, 0, i))]
    else:
        nblk = t_len // MOBA_BLOCK
        in_specs.append(pspec((1, 1, 2 * nblk, tq), lambda bn, hp, i: (bn, i, hp, 0)))
    tok = sspec((1, t_new, WIDTH), lambda bb, s: (bb, 0, 0))
    in_specs += [tok, tok, tok]
    if fox:
        in_specs += [sspec((1, N_HEADS, LANES), lambda bb, s: (bb, 0, 0)),
                     sspec((1, group, N_HEADS, PAGE_SIZE), lambda bb, s: (bb, s, 0, 0))]
    in_specs += [page(g, 0) for g in range(group)] + [page(g, 1) for g in range(group)]
    scratch = [] if online else [pltpu.VMEM((2, 8, tq), F32), pltpu.VMEM((2, HEAD_DIM, tq), F32)]
    if not fox:
        scratch += [pltpu.VMEM((WIDTH, LANES), F32), pltpu.VMEM((rows, pages_per_seq * PAGE_SIZE), F32)]
    scratch += _sample_scratch(t_new)
    grid_spec = pltpu.PrefetchScalarGridSpec(
        num_scalar_prefetch=1,
        grid=(total,),
        in_specs=in_specs,
        out_specs=[pspec((1, tq, LANES), lambda bn, hp, i: (bn, i, hp)), tok],
        scratch_shapes=scratch,
    )
    name = ("fox" if fox else "moba") + "_attn" + ("_online" if online else "")
    return pl.pallas_call(
        functools.partial(_mixer_attn_kernel, fox, online, cfg),
        grid_spec=grid_spec,
        out_shape=[jax.ShapeDtypeStruct((n, t_len, WIDTH), BF16), jax.ShapeDtypeStruct((b, t_new, WIDTH), F32)],
        compiler_params=_cparams(("arbitrary",), 56),
        name=name,
    )(pt_flat, bound, *prompt_args, *sample_args, *([cache_k] * group), *([cache_v] * group))


def _mixer_attn_any_gain(fox, gq, gk, *args):
    slack = 1.02
    limit = 40.0
    bound = (HEAD_DIM ** 0.5) * jnp.max(jnp.abs(gq)) * jnp.max(jnp.abs(gk)) * slack
    bound2 = (bound * LOG2E).astype(F32).reshape(1, 1)
    return lax.cond(bound <= limit,
                    lambda: _mixer_attn(fox, False, bound2, *args),
                    lambda: _mixer_attn(fox, True, bound2, *args))


def _rope_tables(pos):
    half = HEAD_DIM // 2
    inv_freq = ROPE_THETA ** (-jnp.arange(half, dtype=F32) / half)
    ang = pos.astype(F32)[:, None] * inv_freq[None, :]
    cos = jnp.cos(ang)
    sin = jnp.sin(ang)
    cos_t = jnp.tile(cos, (1, LANES // half))
    sin_t = jnp.tile(jnp.concatenate([-sin, sin], axis=1), (1, LANES // HEAD_DIM))
    return cos_t, sin_t


def kernel(x_prompt, x_sample, cache_moba_k, cache_moba_v, cache_fox_k, cache_fox_v, cache_fox_logf, page_table, c_prompt, c_sample, norm_gain, w_ada, b_ada, w_in, b_forget, q_norm_a, k_norm_a, q_norm_b, k_norm_b, w_branch, w_out):
    n, t, d = x_prompt.shape
    b, t_new, _ = x_sample.shape
    depth = w_in.shape[0]
    n_pool = cache_moba_k.shape[1]
    pages = page_table.shape[1]
    past_len = pages * PAGE_SIZE
    assert t % MOBA_BLOCK == 0 and past_len % MOBA_BLOCK == 0 and (b * t_new) % ROW_TILE == 0
    assert ROW_TILE % t_new == 0 and t_new == N_HEADS
    group = min(16, pages)
    d_in = w_in.shape[2]
    w_cols = 12 * WIDTH + LANES

    pos_p = jnp.arange(t)
    pos_s = past_len + jnp.arange(t_new)
    cos_p, sin_p = _rope_tables(pos_p)
    cos_s, sin_s = (jnp.tile(a, (ROW_TILE // t_new, 1)) for a in _rope_tables(pos_s))
    lane_head = jnp.arange(WIDTH) // HEAD_DIM
    gsum = (lane_head[:, None] == lane_head[None, :]).astype(BF16)
    pt_flat = page_table.reshape(-1).astype(jnp.int32)

    yp = x_prompt.reshape(n * t, d)
    ys = x_sample.reshape(b * t_new, d)
    outs_p = [[] for _ in range(5)]
    outs_s = [[] for _ in range(5)]
    for l in range(depth):
        w_pad = jnp.pad(w_in[l], ((0, 0), (0, w_cols - d_in))).astype(BF16)
        bf_pad = jnp.pad(b_forget[l], (0, LANES - N_HEADS)).reshape(1, LANES)
        gains = [jnp.tile(g[l], N_HEADS).reshape(1, WIDTH) for g in (q_norm_a, k_norm_a, q_norm_b, k_norm_b)]
        gain = norm_gain[l].reshape(1, d)
        wb = w_branch[l].astype(BF16)
        wo = w_out[l].astype(BF16)

        mod = _ada_mod(c_prompt, w_ada[l], b_ada[l])
        shift, scale, gate_p = (mod[:, i * d:(i + 1) * d].reshape(n, 1, d) for i in range(3))
        (ka_t, va_t, kb_t, vb_t, logf_t, qa_s, qb_s, ka_b, kb_b, va_tb, vb_tb, za_p, zb_p, bg_p, selb, crep,
         cum_t) = _proj_in(True, yp, shift, scale, gain, w_pad, bf_pad, *gains, cos_p, sin_p, gsum, n, t)
        mod = _ada_mod(c_sample, w_ada[l], b_ada[l])
        rows = lambda a: jnp.repeat(a, t_new, axis=0).reshape(b * t_new // ROW_TILE, ROW_TILE, d)
        shift, scale, gate_s = (rows(mod[:, i * d:(i + 1) * d]) for i in range(3))
        (ka, va, kb, vb, logf, qa, qb, za_s, zb_s, bg_s) = _proj_in(
            False, ys, shift, scale, gain, w_pad, bf_pad, *gains, cos_s, sin_s, gsum, b, t_new)

        as3 = lambda a: a.reshape(n, t, a.shape[-1])
        tok3 = lambda a: a.reshape(b, t_new, WIDTH)
        page_t = lambda c: c[l].transpose(0, 2, 3, 1).reshape(n_pool, WIDTH, PAGE_SIZE)
        ck_a, cv_a, ck_b, cv_b = (page_t(c) for c in (cache_moba_k, cache_moba_v, cache_fox_k, cache_fox_v))
        sfx = _fox_suffix(cache_fox_logf[l].transpose(0, 2, 1), pt_flat, b, pages)
        lfn_t = jnp.pad(logf.reshape(b, t_new, N_HEADS).transpose(0, 2, 1), ((0, 0), (0, 0), (0, LANES - t_new)))
        oa_p, oa_s = _mixer_attn_any_gain(
            False, q_norm_a[l], k_norm_a[l], pt_flat, (as3(qa_s), as3(ka_b), va_tb, selb),
            (tok3(qa), tok3(ka), tok3(va)), ck_a, cv_a, pages, group)
        ob_p, ob_s = _mixer_attn_any_gain(
            True, q_norm_b[l], k_norm_b[l], pt_flat, (as3(qb_s), as3(kb_b), vb_tb, as3(crep), cum_t),
            (tok3(qb), tok3(kb), tok3(vb), lfn_t, sfx), ck_b, cv_b, pages, max(group // 2, 1))

        out_tile = OUT_ROW_TILE if t % OUT_ROW_TILE == 0 else ROW_TILE
        yp = _mixer_out(oa_p.reshape(n * t, WIDTH), ob_p.reshape(n * t, WIDTH), za_p, zb_p, bg_p, yp, gate_p, wb, wo,
                        out_tile, t // out_tile)
        ys = _mixer_out(oa_s.reshape(b * t_new, WIDTH).astype(BF16), ob_s.reshape(b * t_new, WIDTH).astype(BF16),
                        za_s, zb_s, bg_s, ys, gate_s, wb, wo, ROW_TILE, 1)
        for dst, a in zip(outs_p, (ka_t, va_t, kb_t, vb_t)):
            dst.append(a.reshape(n, N_HEADS, HEAD_DIM, t).transpose(0, 3, 1, 2))
        outs_p[4].append(logf_t.transpose(0, 2, 1))
        for dst, a in zip(outs_s, (ka, va, kb, vb)):
            dst.append(a.reshape(b, t_new, N_HEADS, HEAD_DIM))
        outs_s[4].append(logf.reshape(b, t_new, N_HEADS))

    return (yp.reshape(n, t, d), ys.reshape(b, t_new, d),
            *(jnp.stack(o) for o in outs_p), *(jnp.stack(o) for o in outs_s))
```
